```python
import math
import jax, jax.numpy as jnp
from jax import lax
import numpy as np

D_MODEL = 1024
BATCH = 16
SEQ = 2048
DEPTH = 1

ATT_HEADS = 8
ATT_HEAD_DIM = D_MODEL // ATT_HEADS // 2
ATT_V_DIM = 2 * ATT_HEAD_DIM
ATT_WIDTH = ATT_HEADS * ATT_V_DIM
Q_BLOCK = 128
SSM_WIDTH = D_MODEL // 2
SSM_GROUP = 16
SSM_GROUPS = SSM_WIDTH // SSM_GROUP
SSM_STATE = 64
DT_MIN = 1e-3
DT_MAX = 1e-1
N_BRANCHES = 2
SPLIT_SIZES = [ATT_WIDTH] * 4 + [SSM_WIDTH] * 2 + [D_MODEL] * N_BRANCHES
IN_WIDTH = sum(SPLIT_SIZES)
EPS = 1e-5

kernel_name = 'hybrid_diffattn_s5_gated_block'


def rmsnorm(x, g):
    xf = x.astype(jnp.float32)
    y = xf * lax.rsqrt(jnp.mean(xf * xf, axis=-1, keepdims=True) + EPS)
    return (y * g.astype(jnp.float32)).astype(x.dtype)


def alibi_slopes(n_heads):
    return jnp.asarray([2.0 ** (-8.0 * (h + 1) / n_heads) for h in range(n_heads)], dtype=jnp.float32)


def diff_attention(q, k, v, lam, lambda_init, subln_g):
    b, L = q.shape[0], q.shape[1]
    q = jnp.transpose(q, (0, 2, 3, 1, 4)).astype(jnp.float32)
    k = jnp.transpose(k, (0, 2, 3, 1, 4)).astype(jnp.float32)
    v = jnp.transpose(v, (0, 2, 1, 3)).astype(jnp.float32)
    scale = ATT_HEAD_DIM ** -0.5
    slopes = alibi_slopes(ATT_HEADS)[None, :, None, None, None]
    outs = []
    for i in range(L // Q_BLOCK):
        q0, k_end = i * Q_BLOCK, (i + 1) * Q_BLOCK
        qs = q[:, :, :, q0:k_end]
        ks = k[:, :, :, :k_end]
        vs = v[:, :, :k_end]
        s = jnp.einsum('bhcqd,bhckd->bhcqk', qs, ks) * scale
        dist = (jnp.arange(q0, k_end)[:, None] - jnp.arange(k_end)[None, :]).astype(jnp.float32)
        s = jnp.where(dist >= 0, s - slopes * dist, -jnp.inf)
        p = jax.nn.softmax(s, axis=-1)
        pd = p[:, :, 0] - lam * p[:, :, 1]
        outs.append(jnp.einsum('bhqk,bhkv->bhqv', pd, vs))
    o = jnp.concatenate(outs, axis=2)
    o = rmsnorm(o, subln_g) * (1.0 - lambda_init)
    return jnp.transpose(o, (0, 2, 1, 3)).reshape(b, L, ATT_WIDTH)


def s5_branch(u, lam_re, lam_im, log_dt, b_re, b_im, c_re, c_im, d_skip, w_glu, b_glu):
    bsz, L = u.shape[0], u.shape[1]
    uf = u.astype(jnp.float32).reshape(bsz, L, SSM_GROUPS, SSM_GROUP)
    dt = jnp.exp(log_dt.astype(jnp.float32))[:, None]
    lre = jnp.minimum(lam_re.astype(jnp.float32), -1e-4)
    lim = lam_im.astype(jnp.float32)
    mag = jnp.exp(lre * dt)
    lbar_re = mag * jnp.cos(lim * dt)
    lbar_im = mag * jnp.sin(lim * dt)
    num_re = lbar_re - 1.0
    den = lre * lre + lim * lim
    coef_re = ((num_re * lre + lbar_im * lim) / den)[..., None]
    coef_im = ((lbar_im * lre - num_re * lim) / den)[..., None]
    bre = b_re.astype(jnp.float32)
    bim = b_im.astype(jnp.float32)
    bbar_re = coef_re * bre - coef_im * bim
    bbar_im = coef_re * bim + coef_im * bre
    bu_re = jnp.einsum('blgh,gph->blgp', uf, bbar_re)
    bu_im = jnp.einsum('blgh,gph->blgp', uf, bbar_im)
    a_re = jnp.broadcast_to(lbar_re, bu_re.shape)
    a_im = jnp.broadcast_to(lbar_im, bu_im.shape)

    def combine(e1, e2):
        a1r, a1i, b1r, b1i = e1
        a2r, a2i, b2r, b2i = e2
        return (a2r * a1r - a2i * a1i,
                a2r * a1i + a2i * a1r,
                a2r * b1r - a2i * b1i + b2r,
                a2r * b1i + a2i * b1r + b2i)

    _, _, st_re, st_im = lax.associative_scan(combine, (a_re, a_im, bu_re, bu_im), axis=1)
    y = (jnp.einsum('blgp,ghp->blgh', st_re, c_re.astype(jnp.float32))
         - jnp.einsum('blgp,ghp->blgh', st_im, c_im.astype(jnp.float32))
         + d_skip.astype(jnp.float32) * uf)
    y = jax.nn.gelu(y).reshape(bsz, L, SSM_WIDTH)
    y = y * jax.nn.sigmoid(y @ w_glu.astype(jnp.float32) + b_glu.astype(jnp.float32))
    return y.astype(u.dtype)


def setup_inputs(seed: int = 0) -> dict:
    key = jax.random.key(seed)
    ks = jax.random.split(key, 24)
    f32 = jnp.float32
    nrm = lambda k, shape, s: jax.random.normal(k, shape, f32) * s
    n_idx = jnp.arange(SSM_STATE, dtype=f32)
    return {
        'x': jax.random.normal(ks[0], (BATCH, SEQ, D_MODEL), f32),
        'norm_g': 1.0 + nrm(ks[1], (DEPTH, D_MODEL), 0.02),
        'w_in': nrm(ks[2], (DEPTH, D_MODEL, IN_WIDTH), D_MODEL ** -0.5),
        'lambda_q1': nrm(ks[3], (DEPTH, ATT_HEAD_DIM), 0.1),
        'lambda_k1': nrm(ks[4], (DEPTH, ATT_HEAD_DIM), 0.1),
        'lambda_q2': nrm(ks[5], (DEPTH, ATT_HEAD_DIM), 0.1),
        'lambda_k2': nrm(ks[6], (DEPTH, ATT_HEAD_DIM), 0.1),
        'subln_g': 1.0 + nrm(ks[7], (DEPTH, ATT_V_DIM), 0.02),
        'w_o_att': nrm(ks[8], (DEPTH, ATT_WIDTH, D_MODEL), ATT_WIDTH ** -0.5),
        'ssm_lambda_re': -0.5 + nrm(ks[9], (DEPTH, SSM_GROUPS, SSM_STATE), 0.01),
        'ssm_lambda_im': math.pi * n_idx + nrm(ks[10], (DEPTH, SSM_GROUPS, SSM_STATE), 0.01),
        'ssm_log_dt': jax.random.uniform(ks[11], (DEPTH, SSM_GROUPS), f32, math.log(DT_MIN), math.log(DT_MAX)),
        'ssm_b_re': nrm(ks[12], (DEPTH, SSM_GROUPS, SSM_STATE, SSM_GROUP), (2 * SSM_GROUP) ** -0.5),
        'ssm_b_im': nrm(ks[13], (DEPTH, SSM_GROUPS, SSM_STATE, SSM_GROUP), (2 * SSM_GROUP) ** -0.5),
        'ssm_c_re': nrm(ks[14], (DEPTH, SSM_GROUPS, SSM_GROUP, SSM_STATE), SSM_STATE ** -0.5),
        'ssm_c_im': nrm(ks[15], (DEPTH, SSM_GROUPS, SSM_GROUP, SSM_STATE), SSM_STATE ** -0.5),
        'ssm_d': nrm(ks[16], (DEPTH, SSM_GROUPS, SSM_GROUP), 1.0),
        'w_glu': nrm(ks[17], (DEPTH, SSM_WIDTH, SSM_WIDTH), SSM_WIDTH ** -0.5),
        'b_glu': nrm(ks[18], (DEPTH, SSM_WIDTH), 0.01),
        'w_o_ssm': nrm(ks[19], (DEPTH, SSM_WIDTH, D_MODEL), SSM_WIDTH ** -0.5),
        'w_out': nrm(ks[20], (DEPTH, D_MODEL, D_MODEL), D_MODEL ** -0.5),
        'final_g': 1.0 + nrm(ks[21], (D_MODEL,), 0.02),
    }


def reference(x, norm_g, w_in, lambda_q1, lambda_k1, lambda_q2, lambda_k2, subln_g, w_o_att,
              ssm_lambda_re, ssm_lambda_im, ssm_log_dt, ssm_b_re, ssm_b_im, ssm_c_re, ssm_c_im,
              ssm_d, w_glu, b_glu, w_o_ssm, w_out, final_g):
    bsz, L = x.shape[0], x.shape[1]
    split_points = [int(v) for v in np.cumsum(SPLIT_SIZES)[:-1]]
    for l in range(DEPTH):
        lambda_init = 0.8 - 0.6 * math.exp(-0.3 * l)
        h = rmsnorm(x, norm_g[l])
        proj = h @ w_in[l]
        q, k, v, z_att, u, z_ssm, g_att, g_ssm = jnp.split(proj, split_points, axis=-1)
        q = q.reshape(bsz, L, ATT_HEADS, 2, ATT_HEAD_DIM)
        k = k.reshape(bsz, L, ATT_HEADS, 2, ATT_HEAD_DIM)
        v = v.reshape(bsz, L, ATT_HEADS, ATT_V_DIM)
        lam = (jnp.exp(jnp.sum(lambda_q1[l].astype(jnp.float32) * lambda_k1[l].astype(jnp.float32)))
               - jnp.exp(jnp.sum(lambda_q2[l].astype(jnp.float32) * lambda_k2[l].astype(jnp.float32)))
               + lambda_init)
        att = diff_attention(q, k, v, lam, lambda_init, subln_g[l]).astype(x.dtype)
        y_att = (att * jax.nn.silu(z_att)) @ w_o_att[l]
        ssm = s5_branch(u, ssm_lambda_re[l], ssm_lambda_im[l], ssm_log_dt[l], ssm_b_re[l], ssm_b_im[l],
                        ssm_c_re[l], ssm_c_im[l], ssm_d[l], w_glu[l], b_glu[l])
        y_ssm = (ssm * jax.nn.silu(z_ssm)) @ w_o_ssm[l]
        merged = jax.nn.sigmoid(g_att) * y_att + jax.nn.sigmoid(g_ssm) * y_ssm
        x = x + (merged @ w_out[l]).astype(x.dtype)
    return rmsnorm(x, final_g)
```

```python
import functools
import math

import jax
import jax.numpy as jnp
from jax import lax
from jax.experimental import pallas as pl
from jax.experimental.pallas import tpu as pltpu

EPS = 1e-5
ATT_HEADS = 8
ATT_HEAD_DIM = 64
ATT_V_DIM = 2 * ATT_HEAD_DIM
SSM_GROUP = 16
SSM_STATE = 64
LAMBDA_INIT = 0.8 - 0.6 * math.exp(-0.3 * 0)

GROUPS_PER_CHUNK = 8
CHUNK_IN = GROUPS_PER_CHUNK * SSM_GROUP
CHUNK_STATE = GROUPS_PER_CHUNK * SSM_STATE

ROW_TILE = 512
ATT_BLOCK = 256
S5_STEPS = 32
VMEM_LIMIT = 56 * 1024 * 1024

F32 = jnp.float32
BF16 = jnp.bfloat16


def _sigmoid(x):
    return jax.nn.sigmoid(x)


def _silu(x):
    return x * jax.nn.sigmoid(x)


def _in_proj_kernel(x_ref, g_ref, w_ref, qkv_ref, za_ref, u_ref, rest_ref, *, att_w, ssm_w, d_model):
    x = x_ref[...]
    ms = jnp.mean(x * x, axis=-1, keepdims=True)
    h = (x * lax.rsqrt(ms + EPS) * g_ref[...]).astype(BF16)

    def proj(dst_ref, dst0, src0, width, step=512):
        for o in range(0, width, step):
            r = jnp.dot(h, w_ref[:, src0 + o:src0 + o + step], preferred_element_type=F32)
            dst_ref[:, dst0 + o:dst0 + o + step] = r.astype(dst_ref.dtype)

    c = 0
    proj(qkv_ref, 0, c, 3 * att_w); c += 3 * att_w
    proj(za_ref, 0, c, att_w); c += att_w
    proj(u_ref, 0, c, ssm_w); c += ssm_w
    proj(rest_ref, 0, c, ssm_w + 2 * d_model)


def _in_proj(x2, norm_g, w_in, bsz, seq, att_w, ssm_w):
    d_model = x2.shape[1]
    tm = ROW_TILE
    nl = seq // tm
    rest_w = ssm_w + 2 * d_model
    kern = functools.partial(_in_proj_kernel, att_w=att_w, ssm_w=ssm_w, d_model=d_model)
    row = lambda b, l: (b * nl + l, 0)
    return pl.pallas_call(
        kern,
        grid=(bsz, nl),
        in_specs=[
            pl.BlockSpec((tm, d_model), row),
            pl.BlockSpec((1, d_model), lambda b, l: (0, 0)),
            pl.BlockSpec(w_in.shape, lambda b, l: (0, 0)),
        ],
        out_specs=[
            pl.BlockSpec((tm, 3 * att_w), row),
            pl.BlockSpec((tm, att_w), row),
            pl.BlockSpec((tm, ssm_w), lambda b, l: (l, b)),
            pl.BlockSpec((tm, rest_w), row),
        ],
        out_shape=[
            jax.ShapeDtypeStruct((bsz * seq, 3 * att_w), BF16),
            jax.ShapeDtypeStruct((bsz * seq, att_w), BF16),
            jax.ShapeDtypeStruct((seq, bsz * ssm_w), BF16),
            jax.ShapeDtypeStruct((bsz * seq, rest_w), BF16),
        ],
        compiler_params=pltpu.CompilerParams(
            dimension_semantics=("arbitrary", "arbitrary"), vmem_limit_bytes=VMEM_LIMIT),
        name="in_proj",
    )(x2, norm_g, w_in)


def _attn_kernel(slopes_ref, lq1_ref, lk1_ref, lq2_ref, lk2_ref, sg_ref, q_ref, k_ref, v_ref, o_ref,
                 m1_ref, l1_ref, a1_ref, m2_ref, l2_ref, a2_ref):
    h = pl.program_id(1)
    qi = pl.program_id(2)
    blk = ATT_BLOCK
    slope = slopes_ref[h]

    q = q_ref[...]
    lane = lax.broadcasted_iota(jnp.int32, q.shape, 1)
    scale = ATT_HEAD_DIM ** -0.5
    zero = jnp.zeros_like(q)
    q1 = jnp.where(lane < ATT_HEAD_DIM, q, zero) * jnp.asarray(scale, BF16)
    q2 = jnp.where(lane >= ATT_HEAD_DIM, q, zero) * jnp.asarray(scale, BF16)

    neg_inf = jnp.full((blk, 1), -jnp.inf, F32)
    m1_ref[...] = neg_inf
    m2_ref[...] = neg_inf
    l1_ref[...] = jnp.zeros((blk, 1), F32)
    l2_ref[...] = jnp.zeros((blk, 1), F32)
    a1_ref[...] = jnp.zeros((blk, ATT_V_DIM), F32)
    a2_ref[...] = jnp.zeros((blk, ATT_V_DIM), F32)

    col = lax.broadcasted_iota(jnp.int32, (1, blk), 1)
    causal = (lax.broadcasted_iota(jnp.int32, (blk, blk), 1)
              <= lax.broadcasted_iota(jnp.int32, (blk, blk), 0))
    nt = (((1,), (1,)), ((), ()))

    def one_map(qm, k, v, bias, m_ref, l_ref, a_ref, masked):
        s = lax.dot_general(qm, k, nt, preferred_element_type=F32) + bias
        if masked:
            s = jnp.where(causal, s, -jnp.inf)
        m_old = m_ref[...]
        m_new = jnp.maximum(m_old, jnp.max(s, axis=-1, keepdims=True))
        alpha = jnp.exp(m_old - m_new)
        p = jnp.exp(s - m_new)
        l_ref[...] = alpha * l_ref[...] + jnp.sum(p, axis=-1, keepdims=True)
        a_ref[...] = alpha * a_ref[...] + jnp.dot(p.astype(BF16), v, preferred_element_type=F32)
        m_ref[...] = m_new

    def step(j, masked):
        r0 = pl.multiple_of(j * blk, blk)
        k = k_ref[pl.ds(r0, blk), :]
        v = v_ref[pl.ds(r0, blk), :]
        bias = slope * (col + j * blk).astype(F32)
        one_map(q1, k, v, bias, m1_ref, l1_ref, a1_ref, masked)
        one_map(q2, k, v, bias, m2_ref, l2_ref, a2_ref, masked)

    def body(j, carry):
        step(j, False)
        return carry

    lax.fori_loop(0, qi, body, 0)
    step(qi, True)

    e1 = jnp.exp(jnp.sum(lq1_ref[...] * lk1_ref[...], axis=-1, keepdims=True))
    e2 = jnp.exp(jnp.sum(lq2_ref[...] * lk2_ref[...], axis=-1, keepdims=True))
    lam = e1 - e2 + LAMBDA_INIT
    o = a1_ref[...] / l1_ref[...] - lam * (a2_ref[...] / l2_ref[...])
    ms = jnp.mean(o * o, axis=-1, keepdims=True)
    o = o * lax.rsqrt(ms + EPS) * sg_ref[...] * (1.0 - LAMBDA_INIT)
    o_ref[...] = o.astype(o_ref.dtype)


def _attention(qkv, slopes, lq1, lk1, lq2, lk2, subln_g, bsz, seq):
    blk = ATT_BLOCK
    nq = seq // blk
    nh = ATT_HEADS
    vec = pl.BlockSpec((1, ATT_HEAD_DIM), lambda b, h, i: (0, 0))
    return pl.pallas_call(
        _attn_kernel,
        grid=(bsz, nh, nq),
        in_specs=[
            pl.BlockSpec(memory_space=pltpu.SMEM),
            vec, vec, vec, vec,
            pl.BlockSpec((1, ATT_V_DIM), lambda b, h, i: (0, 0)),
            pl.BlockSpec((blk, ATT_V_DIM), lambda b, h, i: (b * nq + i, h)),
            pl.BlockSpec((seq, ATT_V_DIM), lambda b, h, i: (b, nh + h)),
            pl.BlockSpec((seq, ATT_V_DIM), lambda b, h, i: (b, 2 * nh + h)),
        ],
        out_specs=pl.BlockSpec((blk, ATT_V_DIM), lambda b, h, i: (b * nq + i, h)),
        out_shape=jax.ShapeDtypeStruct((bsz * seq, nh * ATT_V_DIM), BF16),
        scratch_shapes=[
            pltpu.VMEM((blk, 1), F32), pltpu.VMEM((blk, 1), F32), pltpu.VMEM((blk, ATT_V_DIM), F32),
            pltpu.VMEM((blk, 1), F32), pltpu.VMEM((blk, 1), F32), pltpu.VMEM((blk, ATT_V_DIM), F32),
        ],
        compiler_params=pltpu.CompilerParams(
            dimension_semantics=("arbitrary", "arbitrary", "arbitrary"), vmem_limit_bytes=VMEM_LIMIT),
        name="diff_attention",
    )(slopes, lq1, lk1, lq2, lk2, subln_g, qkv, qkv, qkv)


def _s5_disc_kernel(lre_ref, lim_ref, ldt_ref, bre_ref, bim_ref, are_ref, aim_ref, bbre_ref, bbim_ref):
    dt = jnp.exp(ldt_ref[...])
    lre = jnp.minimum(lre_ref[...], -1e-4)
    lim = lim_ref[...]
    mag = jnp.exp(lre * dt)
    lbar_re = mag * jnp.cos(lim * dt)
    lbar_im = mag * jnp.sin(lim * dt)
    num_re = lbar_re - 1.0
    den = lre * lre + lim * lim
    coef_re = (num_re * lre + lbar_im * lim) / den
    coef_im = (lbar_im * lre - num_re * lim) / den
    bre = bre_ref[...]
    bim = bim_ref[...]
    are_ref[...] = lbar_re
    aim_ref[...] = lbar_im
    bbre_ref[...] = coef_re * bre - coef_im * bim
    bbim_ref[...] = coef_re * bim + coef_im * bre


def _s5_discretise(lam_re, lam_im, log_dt, b_re, b_im):
    g, p = lam_re.shape
    hg = b_re.shape[-1]
    n = g * p
    row = lambda a: a.reshape(1, n)
    tr = lambda a: a.reshape(n, hg).T
    ldt = jnp.broadcast_to(log_dt[:, None], (g, p))
    return pl.pallas_call(
        _s5_disc_kernel,
        out_shape=[
            jax.ShapeDtypeStruct((1, n), F32), jax.ShapeDtypeStruct((1, n), F32),
            jax.ShapeDtypeStruct((hg, n), F32), jax.ShapeDtypeStruct((hg, n), F32),
        ],
        name="s5_discretise",
    )(row(lam_re), row(lam_im), row(ldt), tr(b_re), tr(b_im))


def _s5_kernel(u_ref, are_ref, aim_ref, wb_ref, wc_ref, d_ref, wglu_ref, bglu_ref, o_ref,
               x_scr, st_scr, *, bsz, steps, n_chunks):
    @pl.when(pl.program_id(0) == 0)
    def _():
        st_scr[...] = jnp.zeros_like(st_scr)

    cs = CHUNK_STATE
    u = u_ref[...]
    for c in range(n_chunks):
        x_scr[:, 2 * cs * c:2 * cs * (c + 1)] = jnp.dot(
            u[:, CHUNK_IN * c:CHUNK_IN * (c + 1)], wb_ref[c], preferred_element_type=F32)

    for c in range(n_chunks):
        re0 = 2 * cs * c
        im0 = re0 + cs
        a_re = jnp.broadcast_to(are_ref[:, cs * c:cs * (c + 1)], (bsz, cs))
        a_im = jnp.broadcast_to(aim_ref[:, cs * c:cs * (c + 1)], (bsz, cs))

        def step(t, carry, re0=re0, im0=im0, a_re=a_re, a_im=a_im):
            xr, xi = carry
            r0 = pl.multiple_of(t * bsz, bsz)
            nr = a_re * xr - a_im * xi + x_scr[pl.ds(r0, bsz), re0:re0 + cs]
            ni = a_re * xi + a_im * xr + x_scr[pl.ds(r0, bsz), im0:im0 + cs]
            x_scr[pl.ds(r0, bsz), re0:re0 + cs] = nr
            x_scr[pl.ds(r0, bsz), im0:im0 + cs] = ni
            return nr, ni

        xr, xi = lax.fori_loop(0, steps, step,
                               (st_scr[:, re0:re0 + cs], st_scr[:, im0:im0 + cs]), unroll=4)
        st_scr[:, re0:re0 + cs] = xr
        st_scr[:, im0:im0 + cs] = xi

    ys = [jnp.dot(x_scr[:, 2 * cs * c:2 * cs * (c + 1)].astype(BF16), wc_ref[c],
                  preferred_element_type=F32) for c in range(n_chunks)]
    y = jnp.concatenate(ys, axis=1) + d_ref[...] * u.astype(F32)
    y = jax.nn.gelu(y)
    z = jnp.dot(y.astype(BF16), wglu_ref[...], preferred_element_type=F32) + bglu_ref[...]
    o_ref[...] = (y * _sigmoid(z)).astype(o_ref.dtype)


def _s5(u_tm, lbar_re, lbar_im, wb, wc, d_row, w_glu, b_glu, bsz, seq):
    ssm_w = u_tm.shape[1]
    n_chunks = wb.shape[0]
    steps = S5_STEPS
    m = steps * bsz
    n_state = n_chunks * CHUNK_STATE
    kern = functools.partial(_s5_kernel, bsz=bsz, steps=steps, n_chunks=n_chunks)
    const2 = lambda i: (0, 0)
    const3 = lambda i: (0, 0, 0)
    return pl.pallas_call(
        kern,
        grid=(seq // steps,),
        in_specs=[
            pl.BlockSpec((m, ssm_w), lambda i: (i, 0)),
            pl.BlockSpec((1, n_state), const2),
            pl.BlockSpec((1, n_state), const2),
            pl.BlockSpec(wb.shape, const3),
            pl.BlockSpec(wc.shape, const3),
            pl.BlockSpec((1, ssm_w), const2),
            pl.BlockSpec(w_glu.shape, const2),
            pl.BlockSpec((1, ssm_w), const2),
        ],
        out_specs=pl.BlockSpec((m, ssm_w), lambda i: (i, 0)),
        out_shape=jax.ShapeDtypeStruct((seq * bsz, ssm_w), BF16),
        scratch_shapes=[
            pltpu.VMEM((m, 2 * n_state), F32),
            pltpu.VMEM((bsz, 2 * n_state), F32),
        ],
        compiler_params=pltpu.CompilerParams(
            dimension_semantics=("arbitrary",), vmem_limit_bytes=VMEM_LIMIT),
        name="s5_scan",
    )(u_tm, lbar_re, lbar_im, wb, wc, d_row, w_glu, b_glu)


def _s5_block_diag_weights(bbar_re, bbar_im, c_re, c_im):
    hg, n = bbar_re.shape
    g = n // SSM_STATE
    nc = g // GROUPS_PER_CHUNK
    eye = jnp.eye(GROUPS_PER_CHUNK, dtype=F32)

    def wb_part(bb):
        t = bb.reshape(hg, nc, GROUPS_PER_CHUNK, SSM_STATE)
        w = jnp.einsum('ab,hcbp->cahbp', eye, t)
        return w.reshape(nc, GROUPS_PER_CHUNK * hg, GROUPS_PER_CHUNK * SSM_STATE)

    def wc_part(cc):
        t = cc.reshape(nc, GROUPS_PER_CHUNK, hg, SSM_STATE)
        w = jnp.einsum('ab,cbhp->capbh', eye, t)
        return w.reshape(nc, GROUPS_PER_CHUNK * SSM_STATE, GROUPS_PER_CHUNK * hg)

    wb = jnp.concatenate([wb_part(bbar_re), wb_part(bbar_im)], axis=2).astype(BF16)
    wc = jnp.concatenate([wc_part(c_re), -wc_part(c_im)], axis=1).astype(BF16)
    return wb, wc


def _out_kernel(att_ref, za_ref, ssm_ref, rest_ref, x_ref, woa_ref, wos_ref, wout_ref, fg_ref, o_ref,
                *, ssm_w, d_model):
    a = att_ref[...].astype(F32) * _silu(za_ref[...].astype(F32))
    ya = jnp.dot(a.astype(BF16), woa_ref[...], preferred_element_type=F32)
    zs = rest_ref[:, 0:ssm_w].astype(F32)
    s = ssm_ref[...].astype(F32) * _silu(zs)
    ys = jnp.dot(s.astype(BF16), wos_ref[...], preferred_element_type=F32)
    ga = rest_ref[:, ssm_w:ssm_w + d_model].astype(F32)
    gs = rest_ref[:, ssm_w + d_model:ssm_w + 2 * d_model].astype(F32)
    merged = _sigmoid(ga) * ya + _sigmoid(gs) * ys
    o = x_ref[...] + jnp.dot(merged.astype(BF16), wout_ref[...], preferred_element_type=F32)
    ms = jnp.mean(o * o, axis=-1, keepdims=True)
    o_ref[...] = o * lax.rsqrt(ms + EPS) * fg_ref[...]


def _out_proj(att, za, ssm_tm, rest, x2, w_o_att, w_o_ssm, w_out, final_g, bsz, seq):
    d_model = x2.shape[1]
    att_w = att.shape[1]
    ssm_w = w_o_ssm.shape[0]
    tm = ROW_TILE
    nl = seq // tm
    row = lambda b, l: (b * nl + l, 0)
    const = lambda b, l: (0, 0)
    kern = functools.partial(_out_kernel, ssm_w=ssm_w, d_model=d_model)
    return pl.pallas_call(
        kern,
        grid=(bsz, nl),
        in_specs=[
            pl.BlockSpec((tm, att_w), row),
            pl.BlockSpec((tm, att_w), row),
            pl.BlockSpec((tm, ssm_w), lambda b, l: (l, b)),
            pl.BlockSpec((tm, rest.shape[1]), row),
            pl.BlockSpec((tm, d_model), row),
            pl.BlockSpec(w_o_att.shape, const),
            pl.BlockSpec(w_o_ssm.shape, const),
            pl.BlockSpec(w_out.shape, const),
            pl.BlockSpec((1, d_model), const),
        ],
        out_specs=pl.BlockSpec((tm, d_model), row),
        out_shape=jax.ShapeDtypeStruct((bsz * seq, d_model), F32),
        compiler_params=pltpu.CompilerParams(
            dimension_semantics=("arbitrary", "arbitrary"), vmem_limit_bytes=VMEM_LIMIT),
        name="out_proj",
    )(att, za, ssm_tm, rest, x2, w_o_att, w_o_ssm, w_out, final_g)


def kernel(x, norm_g, w_in, lambda_q1, lambda_k1, lambda_q2, lambda_k2, subln_g, w_o_att,
           ssm_lambda_re, ssm_lambda_im, ssm_log_dt, ssm_b_re, ssm_b_im, ssm_c_re, ssm_c_im,
           ssm_d, w_glu, b_glu, w_o_ssm, w_out, final_g):
    bsz, seq, d_model = x.shape
    depth = w_in.shape[0]
    assert depth == 1, "single-layer block"
    att_w = w_o_att.shape[1]
    ssm_w = w_o_ssm.shape[1]
    assert att_w == ATT_HEADS * ATT_V_DIM
    assert w_in.shape[2] == 4 * att_w + 2 * ssm_w + 2 * d_model
    assert seq % ROW_TILE == 0 and seq % ATT_BLOCK == 0 and seq % S5_STEPS == 0
    assert bsz % 16 == 0, "batch rows fill whole bf16 sublane tiles in the S5 kernel"
    assert (ssm_w // SSM_GROUP) % GROUPS_PER_CHUNK == 0

    x2 = x.reshape(bsz * seq, d_model)
    qkv, za, u_tm, rest = _in_proj(x2, norm_g[0][None], w_in[0].astype(BF16), bsz, seq, att_w, ssm_w)

    slopes = jnp.asarray([2.0 ** (-8.0 * (h + 1) / ATT_HEADS) for h in range(ATT_HEADS)], F32)
    att = _attention(qkv, slopes, lambda_q1, lambda_k1, lambda_q2, lambda_k2, subln_g, bsz, seq)

    lbar_re, lbar_im, bbar_re, bbar_im = _s5_discretise(
        ssm_lambda_re[0], ssm_lambda_im[0], ssm_log_dt[0], ssm_b_re[0], ssm_b_im[0])
    wb, wc = _s5_block_diag_weights(bbar_re, bbar_im, ssm_c_re[0], ssm_c_im[0])
    ssm = _s5(u_tm.reshape(seq * bsz, ssm_w), lbar_re, lbar_im, wb, wc,
              ssm_d[0].reshape(1, ssm_w), w_glu[0].astype(BF16), b_glu[0][None], bsz, seq)

    out = _out_proj(att, za, ssm.reshape(seq, bsz * ssm_w), rest, x2,
                    w_o_att[0].astype(BF16), w_o_ssm[0].astype(BF16), w_out[0].astype(BF16),
                    final_g[None], bsz, seq)
    return out.reshape(bsz, seq, d_model)
```

```python
import functools
import math

import jax
import jax.numpy as jnp
from jax import lax
from jax.experimental import pallas as pl
from jax.experimental.pallas import tpu as pltpu

EPS = 1e-5
ATT_HEADS = 8
ATT_HEAD_DIM = 64
ATT_V_DIM = 2 * ATT_HEAD_DIM
SSM_GROUP = 16
SSM_STATE = 64
LAMBDA_INIT = 0.8 - 0.6 * math.exp(-0.3 * 0)
LOG2E = math.log2(math.e)
Q_PRESCALE = ATT_HEAD_DIM ** -0.5 * LOG2E
BIAS_TERMS = 3

GROUPS_PER_CHUNK = 8
CHUNK_IN = GROUPS_PER_CHUNK * SSM_GROUP
CHUNK_STATE = GROUPS_PER_CHUNK * SSM_STATE

ROW_TILE = 512
ATT_BLOCK = 256
S5_STEPS = 32
VMEM_LIMIT = 56 * 1024 * 1024

F32 = jnp.float32
BF16 = jnp.bfloat16


def _sigmoid(x):
    return jax.nn.sigmoid(x)


def _silu(x):
    return x * jax.nn.sigmoid(x)


def _in_proj_kernel(x_ref, g_ref, w_ref, qkv_ref, za_ref, u_ref, rest_ref, *, att_w, ssm_w, d_model):
    x = x_ref[...]
    ms = jnp.mean(x * x, axis=-1, keepdims=True)
    h = (x * lax.rsqrt(ms + EPS) * g_ref[...]).astype(BF16)

    def proj(dst_ref, dst0, src0, width, step=512, scale=None):
        for o in range(0, width, step):
            r = jnp.dot(h, w_ref[:, src0 + o:src0 + o + step], preferred_element_type=F32)
            if scale is not None:
                r = r * scale
            dst_ref[:, dst0 + o:dst0 + o + step] = r.astype(dst_ref.dtype)

    c = 0
    proj(qkv_ref, 0, c, att_w, scale=Q_PRESCALE); c += att_w
    proj(qkv_ref, att_w, c, 2 * att_w); c += 2 * att_w
    proj(za_ref, 0, c, att_w); c += att_w
    proj(u_ref, 0, c, ssm_w); c += ssm_w
    proj(rest_ref, 0, c, ssm_w + 2 * d_model)


def _in_proj(x2, norm_g, w_in, bsz, seq, att_w, ssm_w):
    d_model = x2.shape[1]
    tm = ROW_TILE
    nl = seq // tm
    rest_w = ssm_w + 2 * d_model
    kern = functools.partial(_in_proj_kernel, att_w=att_w, ssm_w=ssm_w, d_model=d_model)
    row = lambda b, l: (b * nl + l, 0)
    return pl.pallas_call(
        kern,
        grid=(bsz, nl),
        in_specs=[
            pl.BlockSpec((tm, d_model), row),
            pl.BlockSpec((1, d_model), lambda b, l: (0, 0)),
            pl.BlockSpec(w_in.shape, lambda b, l: (0, 0)),
        ],
        out_specs=[
            pl.BlockSpec((tm, 3 * att_w), row),
            pl.BlockSpec((tm, att_w), row),
            pl.BlockSpec((tm, ssm_w), lambda b, l: (l, b)),
            pl.BlockSpec((tm, rest_w), row),
        ],
        out_shape=[
            jax.ShapeDtypeStruct((bsz * seq, 3 * att_w), BF16),
            jax.ShapeDtypeStruct((bsz * seq, att_w), BF16),
            jax.ShapeDtypeStruct((seq, bsz * ssm_w), BF16),
            jax.ShapeDtypeStruct((bsz * seq, rest_w), BF16),
        ],
        compiler_params=pltpu.CompilerParams(
            dimension_semantics=("arbitrary", "arbitrary"), vmem_limit_bytes=VMEM_LIMIT),
        name="in_proj",
    )(x2, norm_g, w_in)


def _split_bf16(v, n):
    parts = []
    for _ in range(n):
        p = v.astype(BF16)
        parts.append(p)
        v = v - p.astype(F32)
    return parts


def _attn_kernel(slopes_ref, lq1_ref, lk1_ref, lq2_ref, lk2_ref, sg_ref, q_ref, k_ref, v_ref, o_ref,
                 kx_ref, vt_ref, s_ref, p_ref, ot_ref):
    h = pl.program_id(1)
    qi = pl.program_id(2)
    blk = ATT_BLOCK
    hd = ATT_HEAD_DIM
    n_kv = k_ref.shape[0] // blk

    @pl.when(qi == 0)
    def _():
        slope2 = slopes_ref[h] * LOG2E
        for j in range(n_kv):
            rows = pl.ds(j * blk, blk)
            shape = (blk, ATT_V_DIM)
            lane = lax.broadcasted_iota(jnp.int32, shape, 1)
            pos = lax.broadcasted_iota(jnp.int32, shape, 0) + j * blk
            bias = jnp.zeros(shape, F32)
            for t, p in enumerate(_split_bf16(slope2 * pos.astype(F32), BIAS_TERMS)):
                bias = jnp.where(lane == t, p.astype(F32), bias)
            kx_ref[rows, 0:ATT_V_DIM] = k_ref[rows, :]
            kx_ref[rows, ATT_V_DIM:2 * ATT_V_DIM] = bias.astype(BF16)
            vt_ref[:, j * blk:(j + 1) * blk] = v_ref[rows, :].astype(F32).T.astype(BF16)

    q = q_ref[...].astype(F32)
    lane = lax.broadcasted_iota(jnp.int32, q.shape, 1)
    zero = jnp.zeros_like(q)
    ones = jnp.where(lane < BIAS_TERMS, jnp.ones_like(q), zero)
    qx = jnp.concatenate([
        jnp.concatenate([jnp.where(lane < hd, q, zero), ones], axis=1),
        jnp.concatenate([jnp.where(lane >= hd, q, zero), ones], axis=1)], axis=0).astype(BF16)

    e1 = jnp.exp(jnp.sum(lq1_ref[...] * lk1_ref[...], axis=-1, keepdims=True))
    e2 = jnp.exp(jnp.sum(lq2_ref[...] * lk2_ref[...], axis=-1, keepdims=True))
    lam = e1 - e2 + LAMBDA_INIT
    nt = (((1,), (1,)), ((), ()))

    def attend(c):
        nk = (c + 1) * blk
        s_ref[0:nk, :] = lax.dot_general(kx_ref[0:nk, :], qx, nt, preferred_element_type=F32)
        key = lax.broadcasted_iota(jnp.int32, (blk, 2 * blk), 0)
        qry = lax.broadcasted_iota(jnp.int32, (blk, 2 * blk), 1)
        causal = key <= jnp.where(qry >= blk, qry - blk, qry)

        def scores(j):
            s = s_ref[j * blk:(j + 1) * blk, :]
            return jnp.where(causal, s, -jnp.inf) if j == c else s

        m = jnp.max(scores(0), axis=0, keepdims=True)
        for j in range(1, c + 1):
            m = jnp.maximum(m, jnp.max(scores(j), axis=0, keepdims=True))
        l = jnp.zeros_like(m)
        for j in range(c + 1):
            p = jnp.exp2(scores(j) - m)
            l = l + jnp.sum(p, axis=0, keepdims=True)
            p_ref[j * blk:(j + 1) * blk, :] = p.astype(BF16)
        acc = jnp.dot(vt_ref[:, 0:nk], p_ref[0:nk, :], preferred_element_type=F32)
        r = 1.0 / l
        ot_ref[...] = acc[:, 0:blk] * r[:, 0:blk] - acc[:, blk:2 * blk] * (lam * r[:, blk:2 * blk])

    for c in range(n_kv):
        pl.when(qi == c)(functools.partial(attend, c))

    ot = ot_ref[...]
    ms = jnp.mean(ot * ot, axis=0, keepdims=True)
    o = (ot * lax.rsqrt(ms + EPS)).T * (sg_ref[...] * (1.0 - LAMBDA_INIT))
    o_ref[...] = o.astype(o_ref.dtype)


def _attention(qkv, slopes, lq1, lk1, lq2, lk2, subln_g, bsz, seq):
    blk = ATT_BLOCK
    nq = seq // blk
    nh = ATT_HEADS
    vec = pl.BlockSpec((1, ATT_HEAD_DIM), lambda b, h, i: (0, 0))
    return pl.pallas_call(
        _attn_kernel,
        grid=(bsz, nh, nq),
        in_specs=[
            pl.BlockSpec(memory_space=pltpu.SMEM),
            vec, vec, vec, vec,
            pl.BlockSpec((1, ATT_V_DIM), lambda b, h, i: (0, 0)),
            pl.BlockSpec((blk, ATT_V_DIM), lambda b, h, i: (b * nq + i, h)),
            pl.BlockSpec((seq, ATT_V_DIM), lambda b, h, i: (b, nh + h)),
            pl.BlockSpec((seq, ATT_V_DIM), lambda b, h, i: (b, 2 * nh + h)),
        ],
        out_specs=pl.BlockSpec((blk, ATT_V_DIM), lambda b, h, i: (b * nq + i, h)),
        out_shape=jax.ShapeDtypeStruct((bsz * seq, nh * ATT_V_DIM), BF16),
        scratch_shapes=[
            pltpu.VMEM((seq, 2 * ATT_V_DIM), BF16),
            pltpu.VMEM((ATT_V_DIM, seq), BF16),
            pltpu.VMEM((seq, 2 * blk), F32),
            pltpu.VMEM((seq, 2 * blk), BF16),
            pltpu.VMEM((ATT_V_DIM, blk), F32),
        ],
        compiler_params=pltpu.CompilerParams(
            dimension_semantics=("arbitrary", "arbitrary", "arbitrary"), vmem_limit_bytes=VMEM_LIMIT),
        name="diff_attention",
    )(slopes, lq1, lk1, lq2, lk2, subln_g, qkv, qkv, qkv)


def _s5_disc_kernel(lre_ref, lim_ref, ldt_ref, bre_ref, bim_ref, are_ref, aim_ref, bbre_ref, bbim_ref):
    dt = jnp.exp(ldt_ref[...])
    lre = jnp.minimum(lre_ref[...], -1e-4)
    lim = lim_ref[...]
    mag = jnp.exp(lre * dt)
    lbar_re = mag * jnp.cos(lim * dt)
    lbar_im = mag * jnp.sin(lim * dt)
    num_re = lbar_re - 1.0
    den = lre * lre + lim * lim
    coef_re = (num_re * lre + lbar_im * lim) / den
    coef_im = (lbar_im * lre - num_re * lim) / den
    bre = bre_ref[...]
    bim = bim_ref[...]
    are_ref[...] = lbar_re
    aim_ref[...] = lbar_im
    bbre_ref[...] = coef_re * bre - coef_im * bim
    bbim_ref[...] = coef_re * bim + coef_im * bre


def _s5_discretise(lam_re, lam_im, log_dt, b_re, b_im):
    g, p = lam_re.shape
    hg = b_re.shape[-1]
    n = g * p
    row = lambda a: a.reshape(1, n)
    tr = lambda a: a.reshape(n, hg).T
    ldt = jnp.broadcast_to(log_dt[:, None], (g, p))
    return pl.pallas_call(
        _s5_disc_kernel,
        out_shape=[
            jax.ShapeDtypeStruct((1, n), F32), jax.ShapeDtypeStruct((1, n), F32),
            jax.ShapeDtypeStruct((hg, n), F32), jax.ShapeDtypeStruct((hg, n), F32),
        ],
        name="s5_discretise",
    )(row(lam_re), row(lam_im), row(ldt), tr(b_re), tr(b_im))


def _s5_kernel(u_ref, are_ref, aim_ref, wb_ref, wc_ref, d_ref, wglu_ref, bglu_ref, o_ref,
               x_scr, st_scr, *, bsz, steps, n_chunks):
    @pl.when(pl.program_id(0) == 0)
    def _():
        st_scr[...] = jnp.zeros_like(st_scr)

    cs = CHUNK_STATE
    u = u_ref[...]
    for c in range(n_chunks):
        x_scr[:, 2 * cs * c:2 * cs * (c + 1)] = jnp.dot(
            u[:, CHUNK_IN * c:CHUNK_IN * (c + 1)], wb_ref[c], preferred_element_type=F32)

    for c in range(n_chunks):
        re0 = 2 * cs * c
        im0 = re0 + cs
        a_re = jnp.broadcast_to(are_ref[:, cs * c:cs * (c + 1)], (bsz, cs))
        a_im = jnp.broadcast_to(aim_ref[:, cs * c:cs * (c + 1)], (bsz, cs))

        def step(t, carry, re0=re0, im0=im0, a_re=a_re, a_im=a_im):
            xr, xi = carry
            r0 = pl.multiple_of(t * bsz, bsz)
            nr = a_re * xr - a_im * xi + x_scr[pl.ds(r0, bsz), re0:re0 + cs]
            ni = a_re * xi + a_im * xr + x_scr[pl.ds(r0, bsz), im0:im0 + cs]
            x_scr[pl.ds(r0, bsz), re0:re0 + cs] = nr
            x_scr[pl.ds(r0, bsz), im0:im0 + cs] = ni
            return nr, ni

        xr, xi = lax.fori_loop(0, steps, step,
                               (st_scr[:, re0:re0 + cs], st_scr[:, im0:im0 + cs]), unroll=4)
        st_scr[:, re0:re0 + cs] = xr
        st_scr[:, im0:im0 + cs] = xi

    ys = [jnp.dot(x_scr[:, 2 * cs * c:2 * cs * (c + 1)].astype(BF16), wc_ref[c],
                  preferred_element_type=F32) for c in range(n_chunks)]
    y = jnp.concatenate(ys, axis=1) + d_ref[...] * u.astype(F32)
    y = jax.nn.gelu(y)
    z = jnp.dot(y.astype(BF16), wglu_ref[...], preferred_element_type=F32) + bglu_ref[...]
    o_ref[...] = (y * _sigmoid(z)).astype(o_ref.dtype)


def _s5(u_tm, lbar_re, lbar_im, wb, wc, d_row, w_glu, b_glu, bsz, seq):
    ssm_w = u_tm.shape[1]
    n_chunks = wb.shape[0]
    steps = S5_STEPS
    m = steps * bsz
    n_state = n_chunks * CHUNK_STATE
    kern = functools.partial(_s5_kernel, bsz=bsz, steps=steps, n_chunks=n_chunks)
    const2 = lambda i: (0, 0)
    const3 = lambda i: (0, 0, 0)
    return pl.pallas_call(
        kern,
        grid=(seq // steps,),
        in_specs=[
            pl.BlockSpec((m, ssm_w), lambda i: (i, 0)),
            pl.BlockSpec((1, n_state), const2),
            pl.BlockSpec((1, n_state), const2),
            pl.BlockSpec(wb.shape, const3),
            pl.BlockSpec(wc.shape, const3),
            pl.BlockSpec((1, ssm_w), const2),
            pl.BlockSpec(w_glu.shape, const2),
            pl.BlockSpec((1, ssm_w), const2),
        ],
        out_specs=pl.BlockSpec((m, ssm_w), lambda i: (i, 0)),
        out_shape=jax.ShapeDtypeStruct((seq * bsz, ssm_w), BF16),
        scratch_shapes=[
            pltpu.VMEM((m, 2 * n_state), F32),
            pltpu.VMEM((bsz, 2 * n_state), F32),
        ],
        compiler_params=pltpu.CompilerParams(
            dimension_semantics=("arbitrary",), vmem_limit_bytes=VMEM_LIMIT),
        name="s5_scan",
    )(u_tm, lbar_re, lbar_im, wb, wc, d_row, w_glu, b_glu)


def _s5_block_diag_weights(bbar_re, bbar_im, c_re, c_im):
    hg, n = bbar_re.shape
    g = n // SSM_STATE
    nc = g // GROUPS_PER_CHUNK
    eye = jnp.eye(GROUPS_PER_CHUNK, dtype=F32)

    def wb_part(bb):
        t = bb.reshape(hg, nc, GROUPS_PER_CHUNK, SSM_STATE)
        w = jnp.einsum('ab,hcbp->cahbp', eye, t)
        return w.reshape(nc, GROUPS_PER_CHUNK * hg, GROUPS_PER_CHUNK * SSM_STATE)

    def wc_part(cc):
        t = cc.reshape(nc, GROUPS_PER_CHUNK, hg, SSM_STATE)
        w = jnp.einsum('ab,cbhp->capbh', eye, t)
        return w.reshape(nc, GROUPS_PER_CHUNK * SSM_STATE, GROUPS_PER_CHUNK * hg)

    wb = jnp.concatenate([wb_part(bbar_re), wb_part(bbar_im)], axis=2).astype(BF16)
    wc = jnp.concatenate([wc_part(c_re), -wc_part(c_im)], axis=1).astype(BF16)
    return wb, wc


def _out_kernel(att_ref, za_ref, ssm_ref, rest_ref, x_ref, woa_ref, wos_ref, wout_ref, fg_ref, o_ref,
                *, ssm_w, d_model):
    a = att_ref[...].astype(F32) * _silu(za_ref[...].astype(F32))
    ya = jnp.dot(a.astype(BF16), woa_ref[...], preferred_element_type=F32)
    zs = rest_ref[:, 0:ssm_w].astype(F32)
    s = ssm_ref[...].astype(F32) * _silu(zs)
    ys = jnp.dot(s.astype(BF16), wos_ref[...], preferred_element_type=F32)
    ga = rest_ref[:, ssm_w:ssm_w + d_model].astype(F32)
    gs = rest_ref[:, ssm_w + d_model:ssm_w + 2 * d_model].astype(F32)
    merged = _sigmoid(ga) * ya + _sigmoid(gs) * ys
    o = x_ref[...] + jnp.dot(merged.astype(BF16), wout_ref[...], preferred_element_type=F32)
    ms = jnp.mean(o * o, axis=-1, keepdims=True)
    o_ref[...] = o * lax.rsqrt(ms + EPS) * fg_ref[...]


def _out_proj(att, za, ssm_tm, rest, x2, w_o_att, w_o_ssm, w_out, final_g, bsz, seq):
    d_model = x2.shape[1]
    att_w = att.shape[1]
    ssm_w = w_o_ssm.shape[0]
    tm = ROW_TILE
    nl = seq // tm
    row = lambda b, l: (b * nl + l, 0)
    const = lambda b, l: (0, 0)
    kern = functools.partial(_out_kernel, ssm_w=ssm_w, d_model=d_model)
    return pl.pallas_call(
        kern,
        grid=(bsz, nl),
        in_specs=[
            pl.BlockSpec((tm, att_w), row),
            pl.BlockSpec((tm, att_w), row),
            pl.BlockSpec((tm, ssm_w), lambda b, l: (l, b)),
            pl.BlockSpec((tm, rest.shape[1]), row),
            pl.BlockSpec((tm, d_model), row),
            pl.BlockSpec(w_o_att.shape, const),
            pl.BlockSpec(w_o_ssm.shape, const),
            pl.BlockSpec(w_out.shape, const),
            pl.BlockSpec((1, d_model), const),
        ],
        out_specs=pl.BlockSpec((tm, d_model), row),
        out_shape=jax.ShapeDtypeStruct((bsz * seq, d_model), F32),
        compiler_params=pltpu.CompilerParams(
            dimension_semantics=("arbitrary", "arbitrary"), vmem_limit_bytes=VMEM_LIMIT),
        name="out_proj",
    )(att, za, ssm_tm, rest, x2, w_o_att, w_o_ssm, w_out, final_g)


def kernel(x, norm_g, w_in, lambda_q1, lambda_k1, lambda_q2, lambda_k2, subln_g, w_o_att,
           ssm_lambda_re, ssm_lambda_im, ssm_log_dt, ssm_b_re, ssm_b_im, ssm_c_re, ssm_c_im,
           ssm_d, w_glu, b_glu, w_o_ssm, w_out, final_g):
    bsz, seq, d_model = x.shape
    depth = w_in.shape[0]
    assert depth == 1, "single-layer block"
    att_w = w_o_att.shape[1]
    ssm_w = w_o_ssm.shape[1]
    assert att_w == ATT_HEADS * ATT_V_DIM
    assert w_in.shape[2] == 4 * att_w + 2 * ssm_w + 2 * d_model
    assert seq % ROW_TILE == 0 and seq % ATT_BLOCK == 0 and seq % S5_STEPS == 0
    assert bsz % 16 == 0, "batch rows fill whole bf16 sublane tiles in the S5 kernel"
    assert (ssm_w // SSM_GROUP) % GROUPS_PER_CHUNK == 0

    x2 = x.reshape(bsz * seq, d_model)
    qkv, za, u_tm, rest = _in_proj(x2, norm_g[0][None], w_in[0].astype(BF16), bsz, seq, att_w, ssm_w)

    slopes = jnp.asarray([2.0 ** (-8.0 * (h + 1) / ATT_HEADS) for h in range(ATT_HEADS)], F32)
    att = _attention(qkv, slopes, lambda_q1, lambda_k1, lambda_q2, lambda_k2, subln_g, bsz, seq)

    lbar_re, lbar_im, bbar_re, bbar_im = _s5_discretise(
        ssm_lambda_re[0], ssm_lambda_im[0], ssm_log_dt[0], ssm_b_re[0], ssm_b_im[0])
    wb, wc = _s5_block_diag_weights(bbar_re, bbar_im, ssm_c_re[0], ssm_c_im[0])
    ssm = _s5(u_tm.reshape(seq * bsz, ssm_w), lbar_re, lbar_im, wb, wc,
              ssm_d[0].reshape(1, ssm_w), w_glu[0].astype(BF16), b_glu[0][None], bsz, seq)

    out = _out_proj(att, za, ssm.reshape(seq, bsz * ssm_w), rest, x2,
                    w_o_att[0].astype(BF16), w_o_ssm[0].astype(BF16), w_out[0].astype(BF16),
                    final_g[None], bsz, seq)
    return out.reshape(bsz, seq, d_model)
```

```python
import functools
import math

import jax
import jax.numpy as jnp
from jax import lax
from jax.experimental import pallas as pl
from jax.experimental.pallas import tpu as pltpu

EPS = 1e-5
ATT_HEADS = 8
ATT_HEAD_DIM = 64
ATT_V_DIM = 2 * ATT_HEAD_DIM
SSM_GROUP = 16
SSM_STATE = 64
LAMBDA_INIT = 0.8 - 0.6 * math.exp(-0.3 * 0)
LOG2E = math.log2(math.e)
Q_PRESCALE = ATT_HEAD_DIM ** -0.5 * LOG2E
BIAS_TERMS = 3

GROUPS_PER_CHUNK = 8
CHUNK_IN = GROUPS_PER_CHUNK * SSM_GROUP
CHUNK_STATE = GROUPS_PER_CHUNK * SSM_STATE

ROW_TILE = 512
ATT_BLOCK = 256
ATT_SUM_ROWS = 16
S5_STEPS = 32
VMEM_LIMIT = 56 * 1024 * 1024

F32 = jnp.float32
BF16 = jnp.bfloat16


def _sigmoid(x):
    return jax.nn.sigmoid(x)


def _silu(x):
    return x * jax.nn.sigmoid(x)


def _in_proj_kernel(x_ref, g_ref, w_ref, qkv_ref, za_ref, u_ref, rest_ref, *, att_w, ssm_w, d_model):
    x = x_ref[...]
    ms = jnp.mean(x * x, axis=-1, keepdims=True)
    h = (x * lax.rsqrt(ms + EPS) * g_ref[...]).astype(BF16)

    def proj(dst_ref, dst0, src0, width, step=512, scale=None):
        for o in range(0, width, step):
            r = jnp.dot(h, w_ref[:, src0 + o:src0 + o + step], preferred_element_type=F32)
            if scale is not None:
                r = r * scale
            dst_ref[:, dst0 + o:dst0 + o + step] = r.astype(dst_ref.dtype)

    c = 0
    proj(qkv_ref, 0, c, att_w, scale=Q_PRESCALE); c += att_w
    proj(qkv_ref, att_w, c, 2 * att_w); c += 2 * att_w
    proj(za_ref, 0, c, att_w); c += att_w
    proj(u_ref, 0, c, ssm_w); c += ssm_w
    proj(rest_ref, 0, c, ssm_w + 2 * d_model)


def _in_proj(x2, norm_g, w_in, bsz, seq, att_w, ssm_w):
    d_model = x2.shape[1]
    tm = ROW_TILE
    nl = seq // tm
    rest_w = ssm_w + 2 * d_model
    kern = functools.partial(_in_proj_kernel, att_w=att_w, ssm_w=ssm_w, d_model=d_model)
    row = lambda b, l: (b * nl + l, 0)
    return pl.pallas_call(
        kern,
        grid=(bsz, nl),
        in_specs=[
            pl.BlockSpec((tm, d_model), row),
            pl.BlockSpec((1, d_model), lambda b, l: (0, 0)),
            pl.BlockSpec(w_in.shape, lambda b, l: (0, 0)),
        ],
        out_specs=[
            pl.BlockSpec((tm, 3 * att_w), row),
            pl.BlockSpec((tm, att_w), row),
            pl.BlockSpec((tm, ssm_w), row),
            pl.BlockSpec((tm, rest_w), row),
        ],
        out_shape=[
            jax.ShapeDtypeStruct((bsz * seq, 3 * att_w), BF16),
            jax.ShapeDtypeStruct((bsz * seq, att_w), BF16),
            jax.ShapeDtypeStruct((bsz * seq, ssm_w), BF16),
            jax.ShapeDtypeStruct((bsz * seq, rest_w), BF16),
        ],
        compiler_params=pltpu.CompilerParams(
            dimension_semantics=("arbitrary", "arbitrary"), vmem_limit_bytes=VMEM_LIMIT),
        name="in_proj",
    )(x2, norm_g, w_in)


def _split_bf16(v, n):
    parts = []
    for _ in range(n):
        p = v.astype(BF16)
        parts.append(p)
        v = v - p.astype(F32)
    return parts


def _attn_kernel(slopes_ref, lq1_ref, lk1_ref, lq2_ref, lk2_ref, sg_ref, q_ref, k_ref, v_ref, o_ref,
                 kx_ref, vt_ref):
    blk = ATT_BLOCK
    hd = ATT_HEAD_DIM
    vd = ATT_V_DIM
    n_blk = k_ref.shape[0] // blk

    slope2 = slopes_ref[pl.program_id(1)] * LOG2E
    for j in range(n_blk):
        rows = pl.ds(j * blk, blk)
        lane = lax.broadcasted_iota(jnp.int32, (blk, vd), 1)
        pos = lax.broadcasted_iota(jnp.int32, (blk, vd), 0) + j * blk
        bias = jnp.zeros((blk, vd), F32)
        for t, p in enumerate(_split_bf16(slope2 * pos.astype(F32), BIAS_TERMS)):
            bias = jnp.where(lane == t, p.astype(F32), bias)
        kx_ref[rows, 0:vd] = k_ref[rows, :]
        kx_ref[rows, vd:2 * vd] = bias.astype(BF16)
        vt_ref[0:vd, j * blk:(j + 1) * blk] = v_ref[rows, :].astype(F32).T.astype(BF16)
    sub = lax.broadcasted_iota(jnp.int32, (ATT_SUM_ROWS, vt_ref.shape[1]), 0)
    vt_ref[vd:vd + ATT_SUM_ROWS, :] = jnp.where(sub == 0, 1.0, 0.0).astype(BF16)

    e1 = jnp.exp(jnp.sum(lq1_ref[...] * lk1_ref[...], axis=-1, keepdims=True))
    e2 = jnp.exp(jnp.sum(lq2_ref[...] * lk2_ref[...], axis=-1, keepdims=True))
    lam = e1 - e2 + LAMBDA_INIT
    gain = sg_ref[...] * (1.0 - LAMBDA_INIT)
    nt = (((1,), (1,)), ((), ()))

    key = lax.broadcasted_iota(jnp.int32, (blk, 2 * blk), 0)
    qry = lax.broadcasted_iota(jnp.int32, (blk, 2 * blk), 1)
    causal = key <= jnp.where(qry >= blk, qry - blk, qry)
    lane = lax.broadcasted_iota(jnp.int32, (blk, vd), 1)

    order = list(reversed(range(n_blk)))
    scores = {}
    for c in order:
        nk = (c + 1) * blk
        q = q_ref[c * blk:(c + 1) * blk, :].astype(F32)
        zero = jnp.zeros_like(q)
        ones = jnp.where(lane < BIAS_TERMS, jnp.ones_like(q), zero)
        qx = jnp.concatenate([
            jnp.concatenate([jnp.where(lane < hd, q, zero), ones], axis=1),
            jnp.concatenate([jnp.where(lane >= hd, q, zero), ones], axis=1)], axis=0).astype(BF16)
        s = lax.dot_general(kx_ref[0:nk, :], qx, nt, preferred_element_type=F32)
        blocks = [s[j * blk:(j + 1) * blk, :] for j in range(c + 1)]
        blocks[c] = jnp.where(causal, blocks[c], -jnp.inf)
        scores[c] = blocks

    probs = {}
    for c in order:
        m = jnp.max(scores[c][0], axis=0, keepdims=True)
        for j in range(1, c + 1):
            m = jnp.maximum(m, jnp.max(scores[c][j], axis=0, keepdims=True))
        probs[c] = jnp.concatenate(
            [jnp.exp2(scores[c][j] - m).astype(BF16) for j in range(c + 1)], axis=0)

    for c in order:
        nk = (c + 1) * blk
        acc = jnp.dot(vt_ref[:, 0:nk], probs[c], preferred_element_type=F32)
        r = 1.0 / acc[vd:vd + 1, :]
        ot = acc[0:vd, 0:blk] * r[:, 0:blk] - acc[0:vd, blk:2 * blk] * (lam * r[:, blk:2 * blk])
        ms = jnp.mean(ot * ot, axis=0, keepdims=True)
        o_ref[c * blk:(c + 1) * blk, :] = ((ot * lax.rsqrt(ms + EPS)).T * gain).astype(o_ref.dtype)


def _attention(qkv, slopes, lq1, lk1, lq2, lk2, subln_g, bsz, seq):
    blk = ATT_BLOCK
    n_blk = seq // blk
    nh = ATT_HEADS
    vec = pl.BlockSpec((1, ATT_HEAD_DIM), lambda b, h: (0, 0))
    return pl.pallas_call(
        _attn_kernel,
        grid=(bsz, nh),
        in_specs=[
            pl.BlockSpec(memory_space=pltpu.SMEM),
            vec, vec, vec, vec,
            pl.BlockSpec((1, ATT_V_DIM), lambda b, h: (0, 0)),
            pl.BlockSpec((seq, ATT_V_DIM), lambda b, h: (b, h)),
            pl.BlockSpec((seq, ATT_V_DIM), lambda b, h: (b, nh + h)),
            pl.BlockSpec((seq, ATT_V_DIM), lambda b, h: (b, 2 * nh + h)),
        ],
        out_specs=pl.BlockSpec((seq, ATT_V_DIM), lambda b, h: (b, h)),
        out_shape=jax.ShapeDtypeStruct((bsz * seq, nh * ATT_V_DIM), BF16),
        scratch_shapes=[
            pltpu.VMEM((seq, 2 * ATT_V_DIM), BF16),
            pltpu.VMEM((ATT_V_DIM + ATT_SUM_ROWS, seq), BF16),
        ],
        compiler_params=pltpu.CompilerParams(
            dimension_semantics=("arbitrary", "arbitrary"), vmem_limit_bytes=VMEM_LIMIT),
        name="diff_attention",
    )(slopes, lq1, lk1, lq2, lk2, subln_g, qkv, qkv, qkv)


def _s5_disc_kernel(lre_ref, lim_ref, ldt_ref, bre_ref, bim_ref, are_ref, aim_ref, bbre_ref, bbim_ref):
    dt = jnp.exp(ldt_ref[...])
    lre = jnp.minimum(lre_ref[...], -1e-4)
    lim = lim_ref[...]
    mag = jnp.exp(lre * dt)
    lbar_re = mag * jnp.cos(lim * dt)
    lbar_im = mag * jnp.sin(lim * dt)
    num_re = lbar_re - 1.0
    den = lre * lre + lim * lim
    coef_re = (num_re * lre + lbar_im * lim) / den
    coef_im = (lbar_im * lre - num_re * lim) / den
    bre = bre_ref[...]
    bim = bim_ref[...]
    are_ref[...] = lbar_re
    aim_ref[...] = lbar_im
    bbre_ref[...] = coef_re * bre - coef_im * bim
    bbim_ref[...] = coef_re * bim + coef_im * bre


def _s5_discretise(lam_re, lam_im, log_dt, b_re, b_im):
    g, p = lam_re.shape
    hg = b_re.shape[-1]
    n = g * p
    row = lambda a: a.reshape(1, n)
    tr = lambda a: a.reshape(n, hg).T
    ldt = jnp.broadcast_to(log_dt[:, None], (g, p))
    return pl.pallas_call(
        _s5_disc_kernel,
        out_shape=[
            jax.ShapeDtypeStruct((1, n), F32), jax.ShapeDtypeStruct((1, n), F32),
            jax.ShapeDtypeStruct((hg, n), F32), jax.ShapeDtypeStruct((hg, n), F32),
        ],
        name="s5_discretise",
    )(row(lam_re), row(lam_im), row(ldt), tr(b_re), tr(b_im))


def _s5_kernel(u_ref, perm_ref, are_ref, aim_ref, wb_ref, wc_ref, d_ref, wglu_ref, bglu_ref, o_ref,
               x_scr, st_scr, *, bsz, steps, n_chunks):
    @pl.when(pl.program_id(0) == 0)
    def _():
        st_scr[...] = jnp.zeros_like(st_scr)

    cs = CHUNK_STATE
    m = bsz * steps
    perm = perm_ref[...]
    u = jnp.dot(perm, u_ref[...].reshape(m, u_ref.shape[2]), preferred_element_type=F32).astype(BF16)
    for c in range(n_chunks):
        x_scr[:, 2 * cs * c:2 * cs * (c + 1)] = jnp.dot(
            u[:, CHUNK_IN * c:CHUNK_IN * (c + 1)], wb_ref[c], preferred_element_type=F32)

    for c in range(n_chunks):
        re0 = 2 * cs * c
        im0 = re0 + cs
        a_re = jnp.broadcast_to(are_ref[:, cs * c:cs * (c + 1)], (bsz, cs))
        a_im = jnp.broadcast_to(aim_ref[:, cs * c:cs * (c + 1)], (bsz, cs))

        def step(t, carry, re0=re0, im0=im0, a_re=a_re, a_im=a_im):
            xr, xi = carry
            r0 = pl.multiple_of(t * bsz, bsz)
            nr = a_re * xr - a_im * xi + x_scr[pl.ds(r0, bsz), re0:re0 + cs]
            ni = a_re * xi + a_im * xr + x_scr[pl.ds(r0, bsz), im0:im0 + cs]
            x_scr[pl.ds(r0, bsz), re0:re0 + cs] = nr
            x_scr[pl.ds(r0, bsz), im0:im0 + cs] = ni
            return nr, ni

        xr, xi = lax.fori_loop(0, steps, step,
                               (st_scr[:, re0:re0 + cs], st_scr[:, im0:im0 + cs]), unroll=True)
        st_scr[:, re0:re0 + cs] = xr
        st_scr[:, im0:im0 + cs] = xi

    ys = [jnp.dot(x_scr[:, 2 * cs * c:2 * cs * (c + 1)].astype(BF16), wc_ref[c],
                  preferred_element_type=F32) for c in range(n_chunks)]
    y = jnp.concatenate(ys, axis=1) + d_ref[...] * u.astype(F32)
    y = jax.nn.gelu(y)
    z = jnp.dot(y.astype(BF16), wglu_ref[...], preferred_element_type=F32) + bglu_ref[...]
    out = (y * _sigmoid(z)).astype(BF16)
    out = lax.dot_general(perm, out, (((0,), (0,)), ((), ())), preferred_element_type=F32)
    o_ref[...] = out.astype(o_ref.dtype).reshape(o_ref.shape)


def _s5(u_bt, lbar_re, lbar_im, wb, wc, d_row, w_glu, b_glu, bsz, seq):
    ssm_w = u_bt.shape[2]
    n_chunks = wb.shape[0]
    steps = S5_STEPS
    m = steps * bsz
    n_state = n_chunks * CHUNK_STATE
    kern = functools.partial(_s5_kernel, bsz=bsz, steps=steps, n_chunks=n_chunks)
    const2 = lambda i: (0, 0)
    const3 = lambda i: (0, 0, 0)
    r = jnp.arange(m)
    perm = (r[None, :] == ((r % bsz) * steps + r // bsz)[:, None]).astype(BF16)
    return pl.pallas_call(
        kern,
        grid=(seq // steps,),
        in_specs=[
            pl.BlockSpec((bsz, steps, ssm_w), lambda i: (0, i, 0)),
            pl.BlockSpec((m, m), const2),
            pl.BlockSpec((1, n_state), const2),
            pl.BlockSpec((1, n_state), const2),
            pl.BlockSpec(wb.shape, const3),
            pl.BlockSpec(wc.shape, const3),
            pl.BlockSpec((1, ssm_w), const2),
            pl.BlockSpec(w_glu.shape, const2),
            pl.BlockSpec((1, ssm_w), const2),
        ],
        out_specs=pl.BlockSpec((bsz, steps, ssm_w), lambda i: (0, i, 0)),
        out_shape=jax.ShapeDtypeStruct((bsz, seq, ssm_w), BF16),
        scratch_shapes=[
            pltpu.VMEM((m, 2 * n_state), F32),
            pltpu.VMEM((bsz, 2 * n_state), F32),
        ],
        compiler_params=pltpu.CompilerParams(
            dimension_semantics=("arbitrary",), vmem_limit_bytes=VMEM_LIMIT),
        name="s5_scan",
    )(u_bt, perm, lbar_re, lbar_im, wb, wc, d_row, w_glu, b_glu)


def _s5_block_diag_weights(bbar_re, bbar_im, c_re, c_im):
    hg, n = bbar_re.shape
    g = n // SSM_STATE
    nc = g // GROUPS_PER_CHUNK
    eye = jnp.eye(GROUPS_PER_CHUNK, dtype=F32)

    def wb_part(bb):
        t = bb.reshape(hg, nc, GROUPS_PER_CHUNK, SSM_STATE)
        w = jnp.einsum('ab,hcbp->cahbp', eye, t)
        return w.reshape(nc, GROUPS_PER_CHUNK * hg, GROUPS_PER_CHUNK * SSM_STATE)

    def wc_part(cc):
        t = cc.reshape(nc, GROUPS_PER_CHUNK, hg, SSM_STATE)
        w = jnp.einsum('ab,cbhp->capbh', eye, t)
        return w.reshape(nc, GROUPS_PER_CHUNK * SSM_STATE, GROUPS_PER_CHUNK * hg)

    wb = jnp.concatenate([wb_part(bbar_re), wb_part(bbar_im)], axis=2).astype(BF16)
    wc = jnp.concatenate([wc_part(c_re), -wc_part(c_im)], axis=1).astype(BF16)
    return wb, wc


def _out_kernel(att_ref, za_ref, ssm_ref, rest_ref, x_ref, woa_ref, wos_ref, wout_ref, fg_ref, o_ref,
                *, ssm_w, d_model):
    a = att_ref[...].astype(F32) * _silu(za_ref[...].astype(F32))
    ya = jnp.dot(a.astype(BF16), woa_ref[...], preferred_element_type=F32)
    zs = rest_ref[:, 0:ssm_w].astype(F32)
    s = ssm_ref[...].astype(F32) * _silu(zs)
    ys = jnp.dot(s.astype(BF16), wos_ref[...], preferred_element_type=F32)
    ga = rest_ref[:, ssm_w:ssm_w + d_model].astype(F32)
    gs = rest_ref[:, ssm_w + d_model:ssm_w + 2 * d_model].astype(F32)
    merged = _sigmoid(ga) * ya + _sigmoid(gs) * ys
    o = x_ref[...] + jnp.dot(merged.astype(BF16), wout_ref[...], preferred_element_type=F32)
    ms = jnp.mean(o * o, axis=-1, keepdims=True)
    o_ref[...] = o * lax.rsqrt(ms + EPS) * fg_ref[...]


def _out_proj(att, za, ssm, rest, x2, w_o_att, w_o_ssm, w_out, final_g, bsz, seq):
    d_model = x2.shape[1]
    att_w = att.shape[1]
    ssm_w = w_o_ssm.shape[0]
    tm = ROW_TILE
    nl = seq // tm
    row = lambda b, l: (b * nl + l, 0)
    const = lambda b, l: (0, 0)
    kern = functools.partial(_out_kernel, ssm_w=ssm_w, d_model=d_model)
    return pl.pallas_call(
        kern,
        grid=(bsz, nl),
        in_specs=[
            pl.BlockSpec((tm, att_w), row),
            pl.BlockSpec((tm, att_w), row),
            pl.BlockSpec((tm, ssm_w), row),
            pl.BlockSpec((tm, rest.shape[1]), row),
            pl.BlockSpec((tm, d_model), row),
            pl.BlockSpec(w_o_att.shape, const),
            pl.BlockSpec(w_o_ssm.shape, const),
            pl.BlockSpec(w_out.shape, const),
            pl.BlockSpec((1, d_model), const),
        ],
        out_specs=pl.BlockSpec((tm, d_model), row),
        out_shape=jax.ShapeDtypeStruct((bsz * seq, d_model), F32),
        compiler_params=pltpu.CompilerParams(
            dimension_semantics=("arbitrary", "arbitrary"), vmem_limit_bytes=VMEM_LIMIT),
        name="out_proj",
    )(att, za, ssm, rest, x2, w_o_att, w_o_ssm, w_out, final_g)


def kernel(x, norm_g, w_in, lambda_q1, lambda_k1, lambda_q2, lambda_k2, subln_g, w_o_att,
           ssm_lambda_re, ssm_lambda_im, ssm_log_dt, ssm_b_re, ssm_b_im, ssm_c_re, ssm_c_im,
           ssm_d, w_glu, b_glu, w_o_ssm, w_out, final_g):
    bsz, seq, d_model = x.shape
    depth = w_in.shape[0]
    assert depth == 1, "single-layer block"
    att_w = w_o_att.shape[1]
    ssm_w = w_o_ssm.shape[1]
    assert att_w == ATT_HEADS * ATT_V_DIM
    assert w_in.shape[2] == 4 * att_w + 2 * ssm_w + 2 * d_model
    assert seq % ROW_TILE == 0 and seq % ATT_BLOCK == 0 and seq % S5_STEPS == 0
    assert bsz % 16 == 0, "batch rows fill whole bf16 sublane tiles in the S5 kernel"
    assert (ssm_w // SSM_GROUP) % GROUPS_PER_CHUNK == 0

    x2 = x.reshape(bsz * seq, d_model)
    qkv, za, u, rest = _in_proj(x2, norm_g[0][None], w_in[0].astype(BF16), bsz, seq, att_w, ssm_w)

    slopes = jnp.asarray([2.0 ** (-8.0 * (h + 1) / ATT_HEADS) for h in range(ATT_HEADS)], F32)
    att = _attention(qkv, slopes, lambda_q1, lambda_k1, lambda_q2, lambda_k2, subln_g, bsz, seq)

    lbar_re, lbar_im, bbar_re, bbar_im = _s5_discretise(
        ssm_lambda_re[0], ssm_lambda_im[0], ssm_log_dt[0], ssm_b_re[0], ssm_b_im[0])
    wb, wc = _s5_block_diag_weights(bbar_re, bbar_im, ssm_c_re[0], ssm_c_im[0])
    ssm = _s5(u.reshape(bsz, seq, ssm_w), lbar_re, lbar_im, wb, wc,
              ssm_d[0].reshape(1, ssm_w), w_glu[0].astype(BF16), b_glu[0][None], bsz, seq)

    out = _out_proj(att, za, ssm.reshape(bsz * seq, ssm_w), rest, x2,
                    w_o_att[0].astype(BF16), w_o_ssm[0].astype(BF16), w_out[0].astype(BF16),
                    final_g[None], bsz, seq)
    return out.reshape(bsz, seq, d_model)
```

```python
import functools
import math

import jax
import jax.numpy as jnp
from jax import lax
from jax.experimental import pallas as pl
from jax.experimental.pallas import tpu as pltpu

EPS = 1e-5
ATT_HEADS = 8
ATT_HEAD_DIM = 64
ATT_V_DIM = 2 * ATT_HEAD_DIM
SSM_GROUP = 16
SSM_STATE = 64
LAMBDA_INIT = 0.8 - 0.6 * math.exp(-0.3 * 0)
LOG2E = math.log2(math.e)
Q_PRESCALE = ATT_HEAD_DIM ** -0.5 * LOG2E
BIAS_TERMS = 3

GROUPS_PER_CHUNK = 8
CHUNK_IN = GROUPS_PER_CHUNK * SSM_GROUP
CHUNK_STATE = GROUPS_PER_CHUNK * SSM_STATE

ROW_TILE = 1024
ATT_BLOCK = 256
ATT_SUM_ROWS = 16
S5_STEPS = 32
VMEM_LIMIT = 56 * 1024 * 1024

F32 = jnp.float32
BF16 = jnp.bfloat16


def _sigmoid(x):
    return 0.5 * jnp.tanh(0.5 * x) + 0.5


def _silu(x):
    return x * _sigmoid(x)


def _in_proj_kernel(x_ref, g_ref, w_ref, qkv_ref, za_ref, u_ref, rest_ref, *, att_w, ssm_w, d_model):
    x = x_ref[...]
    ms = jnp.mean(x * x, axis=-1, keepdims=True)
    h = (x * lax.rsqrt(ms + EPS) * g_ref[...]).astype(BF16)

    def proj(dst_ref, dst0, src0, width, fn=None, step=512):
        for o in range(0, width, step):
            r = jnp.dot(h, w_ref[:, src0 + o:src0 + o + step], preferred_element_type=F32)
            if fn is not None:
                r = fn(r)
            dst_ref[:, dst0 + o:dst0 + o + step] = r.astype(dst_ref.dtype)

    q0, z0 = 0, 3 * att_w
    u0 = z0 + att_w
    r0 = u0 + ssm_w
    proj(za_ref, 0, z0, att_w, fn=_silu)
    proj(rest_ref, 0, r0, ssm_w, fn=_silu)
    proj(rest_ref, ssm_w, r0 + ssm_w, 2 * d_model, fn=_sigmoid)
    proj(qkv_ref, 0, q0, att_w, fn=lambda r: r * Q_PRESCALE)
    proj(u_ref, 0, u0, ssm_w)
    proj(qkv_ref, att_w, q0 + att_w, 2 * att_w)


def _in_proj(x2, norm_g, w_in, bsz, seq, att_w, ssm_w):
    d_model = x2.shape[1]
    tm = ROW_TILE
    nl = seq // tm
    rest_w = ssm_w + 2 * d_model
    kern = functools.partial(_in_proj_kernel, att_w=att_w, ssm_w=ssm_w, d_model=d_model)
    row = lambda b, l: (b * nl + l, 0)
    return pl.pallas_call(
        kern,
        grid=(bsz, nl),
        in_specs=[
            pl.BlockSpec((tm, d_model), row),
            pl.BlockSpec((1, d_model), lambda b, l: (0, 0)),
            pl.BlockSpec(w_in.shape, lambda b, l: (0, 0), pipeline_mode=pl.Buffered(1)),
        ],
        out_specs=[
            pl.BlockSpec((tm, 3 * att_w), row),
            pl.BlockSpec((tm, att_w), row),
            pl.BlockSpec((tm, ssm_w), row),
            pl.BlockSpec((tm, rest_w), row),
        ],
        out_shape=[
            jax.ShapeDtypeStruct((bsz * seq, 3 * att_w), BF16),
            jax.ShapeDtypeStruct((bsz * seq, att_w), BF16),
            jax.ShapeDtypeStruct((bsz * seq, ssm_w), BF16),
            jax.ShapeDtypeStruct((bsz * seq, rest_w), BF16),
        ],
        compiler_params=pltpu.CompilerParams(
            dimension_semantics=("arbitrary", "arbitrary"), vmem_limit_bytes=VMEM_LIMIT),
        name="in_proj",
    )(x2, norm_g, w_in)


def _split_bf16(v, n):
    parts = []
    for _ in range(n):
        p = v.astype(BF16)
        parts.append(p)
        v = v - p.astype(F32)
    return parts


def _attn_kernel(slopes_ref, lq1_ref, lk1_ref, lq2_ref, lk2_ref, sg_ref, q_ref, k_ref, v_ref, o_ref,
                 kx_ref, vt_ref):
    blk = ATT_BLOCK
    hd = ATT_HEAD_DIM
    vd = ATT_V_DIM
    n_blk = k_ref.shape[0] // blk

    slope2 = slopes_ref[pl.program_id(1)] * LOG2E
    lane = lax.broadcasted_iota(jnp.int32, (blk, vd), 1)
    pos = lax.broadcasted_iota(jnp.int32, (blk, vd), 0)
    local = jnp.zeros((blk, vd), F32)
    for t, p in enumerate(_split_bf16(slope2 * pos.astype(F32), BIAS_TERMS)):
        local = jnp.where(lane == t, p.astype(F32), local)
    lane_row = lax.broadcasted_iota(jnp.int32, (1, vd), 1)
    for j in range(n_blk):
        rows = pl.ds(j * blk, blk)
        offset = jnp.zeros((1, vd), F32)
        for t, p in enumerate(_split_bf16(slope2 * jnp.full((1, vd), j * blk, F32), BIAS_TERMS)):
            offset = jnp.where(lane_row == BIAS_TERMS + t, p.astype(F32), offset)
        kx_ref[rows, 0:vd] = k_ref[rows, :]
        kx_ref[rows, vd:2 * vd] = (local + offset).astype(BF16)
        vt_ref[0:vd, j * blk:(j + 1) * blk] = v_ref[rows, :].astype(F32).T.astype(BF16)
    sub = lax.broadcasted_iota(jnp.int32, (ATT_SUM_ROWS, vt_ref.shape[1]), 0)
    vt_ref[vd:vd + ATT_SUM_ROWS, :] = jnp.where(sub == 0, 1.0, 0.0).astype(BF16)

    e1 = jnp.exp(jnp.sum(lq1_ref[...] * lk1_ref[...], axis=-1, keepdims=True))
    e2 = jnp.exp(jnp.sum(lq2_ref[...] * lk2_ref[...], axis=-1, keepdims=True))
    lam = e1 - e2 + LAMBDA_INIT
    gain = sg_ref[...] * (1.0 - LAMBDA_INIT)
    nt = (((1,), (1,)), ((), ()))

    key = lax.broadcasted_iota(jnp.int32, (blk, 2 * blk), 0)
    qry = lax.broadcasted_iota(jnp.int32, (blk, 2 * blk), 1)
    causal = key <= jnp.where(qry >= blk, qry - blk, qry)
    lane = lax.broadcasted_iota(jnp.int32, (blk, vd), 1)

    order = list(reversed(range(n_blk)))
    scores = {}
    for c in order:
        nk = (c + 1) * blk
        q = q_ref[c * blk:(c + 1) * blk, :].astype(F32)
        zero = jnp.zeros_like(q)
        ones = jnp.where(lane < 2 * BIAS_TERMS, jnp.ones_like(q), zero)
        qx = jnp.concatenate([
            jnp.concatenate([jnp.where(lane < hd, q, zero), ones], axis=1),
            jnp.concatenate([jnp.where(lane >= hd, q, zero), ones], axis=1)], axis=0).astype(BF16)
        s = lax.dot_general(kx_ref[0:nk, :], qx, nt, preferred_element_type=F32)
        blocks = [s[j * blk:(j + 1) * blk, :] for j in range(c + 1)]
        blocks[c] = jnp.where(causal, blocks[c], -jnp.inf)
        scores[c] = blocks

    probs = {}
    for c in order:
        m = jnp.max(scores[c][0], axis=0, keepdims=True)
        for j in range(1, c + 1):
            m = jnp.maximum(m, jnp.max(scores[c][j], axis=0, keepdims=True))
        probs[c] = jnp.concatenate(
            [jnp.exp2(scores[c][j] - m).astype(BF16) for j in range(c + 1)], axis=0)

    for c in order:
        nk = (c + 1) * blk
        acc = jnp.dot(vt_ref[:, 0:nk], probs[c], preferred_element_type=F32)
        r = 1.0 / acc[vd:vd + 1, :]
        ot = acc[0:vd, 0:blk] * r[:, 0:blk] - acc[0:vd, blk:2 * blk] * (lam * r[:, blk:2 * blk])
        ms = jnp.mean(ot * ot, axis=0, keepdims=True)
        o_ref[c * blk:(c + 1) * blk, :] = ((ot * lax.rsqrt(ms + EPS)).T * gain).astype(o_ref.dtype)


def _attention(qkv, slopes, lq1, lk1, lq2, lk2, subln_g, bsz, seq):
    blk = ATT_BLOCK
    n_blk = seq // blk
    nh = ATT_HEADS
    vec = pl.BlockSpec((1, ATT_HEAD_DIM), lambda b, h: (0, 0))
    return pl.pallas_call(
        _attn_kernel,
        grid=(bsz, nh),
        in_specs=[
            pl.BlockSpec(memory_space=pltpu.SMEM),
            vec, vec, vec, vec,
            pl.BlockSpec((1, ATT_V_DIM), lambda b, h: (0, 0)),
            pl.BlockSpec((seq, ATT_V_DIM), lambda b, h: (b, h)),
            pl.BlockSpec((seq, ATT_V_DIM), lambda b, h: (b, nh + h)),
            pl.BlockSpec((seq, ATT_V_DIM), lambda b, h: (b, 2 * nh + h)),
        ],
        out_specs=pl.BlockSpec((seq, ATT_V_DIM), lambda b, h: (b, h)),
        out_shape=jax.ShapeDtypeStruct((bsz * seq, nh * ATT_V_DIM), BF16),
        scratch_shapes=[
            pltpu.VMEM((seq, 2 * ATT_V_DIM), BF16),
            pltpu.VMEM((ATT_V_DIM + ATT_SUM_ROWS, seq), BF16),
        ],
        compiler_params=pltpu.CompilerParams(
            dimension_semantics=("arbitrary", "arbitrary"), vmem_limit_bytes=VMEM_LIMIT),
        name="diff_attention",
    )(slopes, lq1, lk1, lq2, lk2, subln_g, qkv, qkv, qkv)


def _s5_disc_kernel(lre_ref, lim_ref, ldt_ref, bre_ref, bim_ref, are_ref, aim_ref, bbre_ref, bbim_ref):
    dt = jnp.exp(ldt_ref[...])
    lre = jnp.minimum(lre_ref[...], -1e-4)
    lim = lim_ref[...]
    mag = jnp.exp(lre * dt)
    lbar_re = mag * jnp.cos(lim * dt)
    lbar_im = mag * jnp.sin(lim * dt)
    num_re = lbar_re - 1.0
    den = lre * lre + lim * lim
    coef_re = (num_re * lre + lbar_im * lim) / den
    coef_im = (lbar_im * lre - num_re * lim) / den
    bre = bre_ref[...]
    bim = bim_ref[...]
    are_ref[...] = lbar_re
    aim_ref[...] = lbar_im
    bbre_ref[...] = coef_re * bre - coef_im * bim
    bbim_ref[...] = coef_re * bim + coef_im * bre


def _s5_discretise(lam_re, lam_im, log_dt, b_re, b_im):
    g, p = lam_re.shape
    hg = b_re.shape[-1]
    n = g * p
    row = lambda a: a.reshape(1, n)
    tr = lambda a: a.reshape(n, hg).T
    ldt = jnp.broadcast_to(log_dt[:, None], (g, p))
    return pl.pallas_call(
        _s5_disc_kernel,
        out_shape=[
            jax.ShapeDtypeStruct((1, n), F32), jax.ShapeDtypeStruct((1, n), F32),
            jax.ShapeDtypeStruct((hg, n), F32), jax.ShapeDtypeStruct((hg, n), F32),
        ],
        name="s5_discretise",
    )(row(lam_re), row(lam_im), row(ldt), tr(b_re), tr(b_im))


def _s5_kernel(u_ref, perm_ref, are_ref, aim_ref, wb_ref, wc_ref, d_ref, wglu_ref, bglu_ref, o_ref,
               x_scr, st_scr, *, bsz, steps, n_chunks):
    @pl.when(pl.program_id(0) == 0)
    def _():
        st_scr[...] = jnp.zeros_like(st_scr)

    cs = CHUNK_STATE
    m = bsz * steps
    perm = perm_ref[...]
    u = jnp.dot(perm, u_ref[...].reshape(m, u_ref.shape[2]), preferred_element_type=F32).astype(BF16)
    for c in range(n_chunks):
        x_scr[:, 2 * cs * c:2 * cs * (c + 1)] = jnp.dot(
            u[:, CHUNK_IN * c:CHUNK_IN * (c + 1)], wb_ref[c], preferred_element_type=F32)

    for c in range(n_chunks):
        re0 = 2 * cs * c
        im0 = re0 + cs
        a_re = jnp.broadcast_to(are_ref[:, cs * c:cs * (c + 1)], (bsz, cs))
        a_im = jnp.broadcast_to(aim_ref[:, cs * c:cs * (c + 1)], (bsz, cs))

        def step(t, carry, re0=re0, im0=im0, a_re=a_re, a_im=a_im):
            xr, xi = carry
            r0 = pl.multiple_of(t * bsz, bsz)
            nr = a_re * xr - a_im * xi + x_scr[pl.ds(r0, bsz), re0:re0 + cs]
            ni = a_re * xi + a_im * xr + x_scr[pl.ds(r0, bsz), im0:im0 + cs]
            x_scr[pl.ds(r0, bsz), re0:re0 + cs] = nr
            x_scr[pl.ds(r0, bsz), im0:im0 + cs] = ni
            return nr, ni

        xr, xi = lax.fori_loop(0, steps, step,
                               (st_scr[:, re0:re0 + cs], st_scr[:, im0:im0 + cs]), unroll=True)
        st_scr[:, re0:re0 + cs] = xr
        st_scr[:, im0:im0 + cs] = xi

    ys = [jnp.dot(x_scr[:, 2 * cs * c:2 * cs * (c + 1)].astype(BF16), wc_ref[c],
                  preferred_element_type=F32) for c in range(n_chunks)]
    y = jnp.concatenate(ys, axis=1) + d_ref[...] * u.astype(F32)
    y = jax.nn.gelu(y)
    z = jnp.dot(y.astype(BF16), wglu_ref[...], preferred_element_type=F32) + bglu_ref[...]
    out = (y * _sigmoid(z)).astype(BF16)
    out = lax.dot_general(perm, out, (((0,), (0,)), ((), ())), preferred_element_type=F32)
    o_ref[...] = out.astype(o_ref.dtype).reshape(o_ref.shape)


def _s5(u_bt, lbar_re, lbar_im, wb, wc, d_row, w_glu, b_glu, bsz, seq):
    ssm_w = u_bt.shape[2]
    n_chunks = wb.shape[0]
    steps = S5_STEPS
    m = steps * bsz
    n_state = n_chunks * CHUNK_STATE
    kern = functools.partial(_s5_kernel, bsz=bsz, steps=steps, n_chunks=n_chunks)
    const2 = lambda i: (0, 0)
    const3 = lambda i: (0, 0, 0)
    r = jnp.arange(m)
    perm = (r[None, :] == ((r % bsz) * steps + r // bsz)[:, None]).astype(BF16)
    return pl.pallas_call(
        kern,
        grid=(seq // steps,),
        in_specs=[
            pl.BlockSpec((bsz, steps, ssm_w), lambda i: (0, i, 0)),
            pl.BlockSpec((m, m), const2),
            pl.BlockSpec((1, n_state), const2),
            pl.BlockSpec((1, n_state), const2),
            pl.BlockSpec(wb.shape, const3),
            pl.BlockSpec(wc.shape, const3),
            pl.BlockSpec((1, ssm_w), const2),
            pl.BlockSpec(w_glu.shape, const2),
            pl.BlockSpec((1, ssm_w), const2),
        ],
        out_specs=pl.BlockSpec((bsz, steps, ssm_w), lambda i: (0, i, 0)),
        out_shape=jax.ShapeDtypeStruct((bsz, seq, ssm_w), BF16),
        scratch_shapes=[
            pltpu.VMEM((m, 2 * n_state), F32),
            pltpu.VMEM((bsz, 2 * n_state), F32),
        ],
        compiler_params=pltpu.CompilerParams(
            dimension_semantics=("arbitrary",), vmem_limit_bytes=VMEM_LIMIT),
        name="s5_scan",
    )(u_bt, perm, lbar_re, lbar_im, wb, wc, d_row, w_glu, b_glu)


def _s5_block_diag_weights(bbar_re, bbar_im, c_re, c_im):
    hg, n = bbar_re.shape
    g = n // SSM_STATE
    nc = g // GROUPS_PER_CHUNK
    eye = jnp.eye(GROUPS_PER_CHUNK, dtype=F32)

    def wb_part(bb):
        t = bb.reshape(hg, nc, GROUPS_PER_CHUNK, SSM_STATE)
        w = jnp.einsum('ab,hcbp->cahbp', eye, t)
        return w.reshape(nc, GROUPS_PER_CHUNK * hg, GROUPS_PER_CHUNK * SSM_STATE)

    def wc_part(cc):
        t = cc.reshape(nc, GROUPS_PER_CHUNK, hg, SSM_STATE)
        w = jnp.einsum('ab,cbhp->capbh', eye, t)
        return w.reshape(nc, GROUPS_PER_CHUNK * SSM_STATE, GROUPS_PER_CHUNK * hg)

    wb = jnp.concatenate([wb_part(bbar_re), wb_part(bbar_im)], axis=2).astype(BF16)
    wc = jnp.concatenate([wc_part(c_re), -wc_part(c_im)], axis=1).astype(BF16)
    return wb, wc


def _out_kernel(att_ref, za_ref, ssm_ref, rest_ref, x_ref, woa_ref, wos_ref, wout_ref, fg_ref, o_ref,
                *, ssm_w, d_model):
    ya = jnp.dot(att_ref[...] * za_ref[...], woa_ref[...], preferred_element_type=F32)
    ys = jnp.dot(ssm_ref[...] * rest_ref[:, 0:ssm_w], wos_ref[...], preferred_element_type=F32)
    ga = rest_ref[:, ssm_w:ssm_w + d_model].astype(F32)
    gs = rest_ref[:, ssm_w + d_model:ssm_w + 2 * d_model].astype(F32)
    merged = ga * ya + gs * ys
    o = x_ref[...] + jnp.dot(merged.astype(BF16), wout_ref[...], preferred_element_type=F32)
    ms = jnp.mean(o * o, axis=-1, keepdims=True)
    o_ref[...] = o * lax.rsqrt(ms + EPS) * fg_ref[...]


def _out_proj(att, za, ssm, rest, x2, w_o_att, w_o_ssm, w_out, final_g, bsz, seq):
    d_model = x2.shape[1]
    att_w = att.shape[1]
    ssm_w = w_o_ssm.shape[0]
    tm = ROW_TILE
    nl = seq // tm
    row = lambda b, l: (b * nl + l, 0)
    const = lambda b, l: (0, 0)
    kern = functools.partial(_out_kernel, ssm_w=ssm_w, d_model=d_model)
    return pl.pallas_call(
        kern,
        grid=(bsz, nl),
        in_specs=[
            pl.BlockSpec((tm, att_w), row),
            pl.BlockSpec((tm, att_w), row),
            pl.BlockSpec((tm, ssm_w), row),
            pl.BlockSpec((tm, rest.shape[1]), row),
            pl.BlockSpec((tm, d_model), row),
            pl.BlockSpec(w_o_att.shape, const),
            pl.BlockSpec(w_o_ssm.shape, const),
            pl.BlockSpec(w_out.shape, const),
            pl.BlockSpec((1, d_model), const),
        ],
        out_specs=pl.BlockSpec((tm, d_model), row),
        out_shape=jax.ShapeDtypeStruct((bsz * seq, d_model), F32),
        compiler_params=pltpu.CompilerParams(
            dimension_semantics=("arbitrary", "arbitrary"), vmem_limit_bytes=VMEM_LIMIT),
        name="out_proj",
    )(att, za, ssm, rest, x2, w_o_att, w_o_ssm, w_out, final_g)


def kernel(x, norm_g, w_in, lambda_q1, lambda_k1, lambda_q2, lambda_k2, subln_g, w_o_att,
           ssm_lambda_re, ssm_lambda_im, ssm_log_dt, ssm_b_re, ssm_b_im, ssm_c_re, ssm_c_im,
           ssm_d, w_glu, b_glu, w_o_ssm, w_out, final_g):
    bsz, seq, d_model = x.shape
    depth = w_in.shape[0]
    assert depth == 1, "single-layer block"
    att_w = w_o_att.shape[1]
    ssm_w = w_o_ssm.shape[1]
    assert att_w == ATT_HEADS * ATT_V_DIM
    assert w_in.shape[2] == 4 * att_w + 2 * ssm_w + 2 * d_model
    assert seq % ROW_TILE == 0 and seq % ATT_BLOCK == 0 and seq % S5_STEPS == 0
    assert bsz % 16 == 0, "batch rows fill whole bf16 sublane tiles in the S5 kernel"
    assert (ssm_w // SSM_GROUP) % GROUPS_PER_CHUNK == 0

    x2 = x.reshape(bsz * seq, d_model)
    qkv, za, u, rest = _in_proj(x2, norm_g[0][None], w_in[0].astype(BF16), bsz, seq, att_w, ssm_w)

    slopes = jnp.asarray([2.0 ** (-8.0 * (h + 1) / ATT_HEADS) for h in range(ATT_HEADS)], F32)
    att = _attention(qkv, slopes, lambda_q1, lambda_k1, lambda_q2, lambda_k2, subln_g, bsz, seq)

    lbar_re, lbar_im, bbar_re, bbar_im = _s5_discretise(
        ssm_lambda_re[0], ssm_lambda_im[0], ssm_log_dt[0], ssm_b_re[0], ssm_b_im[0])
    wb, wc = _s5_block_diag_weights(bbar_re, bbar_im, ssm_c_re[0], ssm_c_im[0])
    ssm = _s5(u.reshape(bsz, seq, ssm_w), lbar_re, lbar_im, wb, wc,
              ssm_d[0].reshape(1, ssm_w), w_glu[0].astype(BF16), b_glu[0][None], bsz, seq)

    out = _out_proj(att, za, ssm.reshape(bsz * seq, ssm_w), rest, x2,
                    w_o_att[0].astype(BF16), w_o_ssm[0].astype(BF16), w_out[0].astype(BF16),
                    final_g[None], bsz, seq)
    return out.reshape(bsz, seq, d_model)
```

```python
import functools
import math

import jax
import jax.numpy as jnp
from jax import lax
from jax.experimental import pallas as pl
from jax.experimental.pallas import tpu as pltpu

EPS = 1e-5
ATT_HEADS = 8
ATT_HEAD_DIM = 64
ATT_V_DIM = 2 * ATT_HEAD_DIM
SSM_GROUP = 16
SSM_STATE = 64
LAMBDA_INIT = 0.8 - 0.6 * math.exp(-0.3 * 0)
LOG2E = math.log2(math.e)
Q_PRESCALE = ATT_HEAD_DIM ** -0.5 * LOG2E
BIAS_TERMS = 3

GROUPS_PER_CHUNK = 8
CHUNK_IN = GROUPS_PER_CHUNK * SSM_GROUP
CHUNK_STATE = GROUPS_PER_CHUNK * SSM_STATE

ROW_TILE = 1024
ATT_BLOCK = 256
ATT_SUM_ROWS = 16
S5_STEPS = 32
VMEM_LIMIT = 56 * 1024 * 1024

F32 = jnp.float32
BF16 = jnp.bfloat16


def _sigmoid(x):
    return 0.5 * jnp.tanh(0.5 * x) + 0.5


def _silu(x):
    return x * _sigmoid(x)


def _in_proj_kernel(x_ref, g_ref, w_ref, qkv_ref, za_ref, u_ref, rest_ref, *, att_w, ssm_w, d_model):
    x = x_ref[...]
    ms = jnp.mean(x * x, axis=-1, keepdims=True)
    h = (x * lax.rsqrt(ms + EPS) * g_ref[...]).astype(BF16)

    def proj(dst_ref, dst0, src0, width, fn=None, step=512):
        for o in range(0, width, step):
            r = jnp.dot(h, w_ref[:, src0 + o:src0 + o + step], preferred_element_type=F32)
            if fn is not None:
                r = fn(r)
            dst_ref[:, dst0 + o:dst0 + o + step] = r.astype(dst_ref.dtype)

    q0, z0 = 0, 3 * att_w
    u0 = z0 + att_w
    r0 = u0 + ssm_w
    proj(za_ref, 0, z0, att_w, fn=_silu)
    proj(rest_ref, 0, r0, ssm_w, fn=_silu)
    proj(rest_ref, ssm_w, r0 + ssm_w, 2 * d_model, fn=_sigmoid)
    proj(qkv_ref, 0, q0, att_w, fn=lambda r: r * Q_PRESCALE)
    proj(u_ref, 0, u0, ssm_w)
    proj(qkv_ref, att_w, q0 + att_w, 2 * att_w)


def _in_proj(x2, norm_g, w_in, bsz, seq, att_w, ssm_w):
    d_model = x2.shape[1]
    tm = ROW_TILE
    nl = seq // tm
    rest_w = ssm_w + 2 * d_model
    kern = functools.partial(_in_proj_kernel, att_w=att_w, ssm_w=ssm_w, d_model=d_model)
    row = lambda b, l: (b * nl + l, 0)
    return pl.pallas_call(
        kern,
        grid=(bsz, nl),
        in_specs=[
            pl.BlockSpec((tm, d_model), row),
            pl.BlockSpec((1, d_model), lambda b, l: (0, 0)),
            pl.BlockSpec(w_in.shape, lambda b, l: (0, 0), pipeline_mode=pl.Buffered(1)),
        ],
        out_specs=[
            pl.BlockSpec((tm, 3 * att_w), row),
            pl.BlockSpec((tm, att_w), row),
            pl.BlockSpec((tm, ssm_w), row),
            pl.BlockSpec((tm, rest_w), row),
        ],
        out_shape=[
            jax.ShapeDtypeStruct((bsz * seq, 3 * att_w), BF16),
            jax.ShapeDtypeStruct((bsz * seq, att_w), BF16),
            jax.ShapeDtypeStruct((bsz * seq, ssm_w), BF16),
            jax.ShapeDtypeStruct((bsz * seq, rest_w), BF16),
        ],
        compiler_params=pltpu.CompilerParams(
            dimension_semantics=("arbitrary", "arbitrary"), vmem_limit_bytes=VMEM_LIMIT),
        name="in_proj",
    )(x2, norm_g, w_in)


def _split_bf16(v, n):
    parts = []
    for _ in range(n):
        p = v.astype(BF16)
        parts.append(p)
        v = v - p.astype(F32)
    return parts


def _attn_kernel(slopes_ref, zero_ref, lq1_ref, lk1_ref, lq2_ref, lk2_ref, sg_ref, q_ref, k_ref, v_ref, o_ref,
                 kx_ref, vt_ref, *sp_refs):
    blk = ATT_BLOCK
    hd = ATT_HEAD_DIM
    vd = ATT_V_DIM
    n_blk = k_ref.shape[0] // blk
    s_refs, p_refs = sp_refs[:n_blk], sp_refs[n_blk:]

    slope2 = slopes_ref[pl.program_id(1)] * LOG2E
    lane = lax.broadcasted_iota(jnp.int32, (blk, vd), 1)
    pos = lax.broadcasted_iota(jnp.int32, (blk, vd), 0)
    local = jnp.zeros((blk, vd), F32)
    for t, p in enumerate(_split_bf16(slope2 * pos.astype(F32), BIAS_TERMS)):
        local = jnp.where(lane == t, p.astype(F32), local)
    lane_row = lax.broadcasted_iota(jnp.int32, (1, vd), 1)
    for j in range(n_blk):
        rows = pl.ds(j * blk, blk)
        offset = jnp.zeros((1, vd), F32)
        for t, p in enumerate(_split_bf16(slope2 * jnp.full((1, vd), j * blk, F32), BIAS_TERMS)):
            offset = jnp.where(lane_row == BIAS_TERMS + t, p.astype(F32), offset)
        kx_ref[rows, 0:vd] = k_ref[rows, :]
        kx_ref[rows, vd:2 * vd] = (local + offset).astype(BF16)
        vt_ref[0:vd, j * blk:(j + 1) * blk] = v_ref[rows, :].astype(F32).T.astype(BF16)
    sub = lax.broadcasted_iota(jnp.int32, (ATT_SUM_ROWS, vt_ref.shape[1]), 0)
    vt_ref[vd:vd + ATT_SUM_ROWS, :] = jnp.where(sub == 0, 1.0, 0.0).astype(BF16)

    e1 = jnp.exp(jnp.sum(lq1_ref[...] * lk1_ref[...], axis=-1, keepdims=True))
    e2 = jnp.exp(jnp.sum(lq2_ref[...] * lk2_ref[...], axis=-1, keepdims=True))
    lam = e1 - e2 + LAMBDA_INIT
    gain = sg_ref[...] * (1.0 - LAMBDA_INIT)
    nt = (((1,), (1,)), ((), ()))

    key = lax.broadcasted_iota(jnp.int32, (blk, 2 * blk), 0)
    qry = lax.broadcasted_iota(jnp.int32, (blk, 2 * blk), 1)
    causal = key <= jnp.where(qry >= blk, qry - blk, qry)
    lane = lax.broadcasted_iota(jnp.int32, (blk, vd), 1)

    dyn0 = zero_ref[0]
    order = list(reversed(range(n_blk)))
    col_max = {}
    for c in order:
        nk = (c + 1) * blk
        q = q_ref[c * blk:(c + 1) * blk, :].astype(F32)
        zero = jnp.zeros_like(q)
        ones = jnp.where(lane < 2 * BIAS_TERMS, jnp.ones_like(q), zero)
        qx = jnp.concatenate([
            jnp.concatenate([jnp.where(lane < hd, q, zero), ones], axis=1),
            jnp.concatenate([jnp.where(lane >= hd, q, zero), ones], axis=1)], axis=0).astype(BF16)
        s = lax.dot_general(kx_ref[0:nk, :], qx, nt, preferred_element_type=F32)
        m = None
        for j in range(c + 1):
            sj = s[j * blk:(j + 1) * blk, :]
            if j == c:
                sj = jnp.where(causal, sj, -jnp.inf)
            s_refs[c][j * blk:(j + 1) * blk, :] = sj
            mj = jnp.max(sj, axis=0, keepdims=True)
            m = mj if m is None else jnp.maximum(m, mj)
        col_max[c] = m

    for c in order:
        for j in range(c + 1):
            sj = s_refs[c][pl.ds(pl.multiple_of(dyn0 + j * blk, blk), blk), :]
            p_refs[c][j * blk:(j + 1) * blk, :] = jnp.exp2(sj - col_max[c]).astype(BF16)

    for c in order:
        nk = (c + 1) * blk
        p = p_refs[c][pl.ds(pl.multiple_of(dyn0, blk), nk), :]
        acc = jnp.dot(vt_ref[:, 0:nk], p, preferred_element_type=F32)
        r = 1.0 / acc[vd:vd + 1, :]
        ot = acc[0:vd, 0:blk] * r[:, 0:blk] - acc[0:vd, blk:2 * blk] * (lam * r[:, blk:2 * blk])
        ms = jnp.mean(ot * ot, axis=0, keepdims=True)
        o_ref[c * blk:(c + 1) * blk, :] = ((ot * lax.rsqrt(ms + EPS)).T * gain).astype(o_ref.dtype)


def _attention(qkv, slopes, lq1, lk1, lq2, lk2, subln_g, bsz, seq):
    blk = ATT_BLOCK
    n_blk = seq // blk
    nh = ATT_HEADS
    vec = pl.BlockSpec((1, ATT_HEAD_DIM), lambda b, h: (0, 0))
    return pl.pallas_call(
        _attn_kernel,
        grid=(bsz, nh),
        in_specs=[
            pl.BlockSpec(memory_space=pltpu.SMEM),
            pl.BlockSpec(memory_space=pltpu.SMEM),
            vec, vec, vec, vec,
            pl.BlockSpec((1, ATT_V_DIM), lambda b, h: (0, 0)),
            pl.BlockSpec((seq, ATT_V_DIM), lambda b, h: (b, h)),
            pl.BlockSpec((seq, ATT_V_DIM), lambda b, h: (b, nh + h)),
            pl.BlockSpec((seq, ATT_V_DIM), lambda b, h: (b, 2 * nh + h)),
        ],
        out_specs=pl.BlockSpec((seq, ATT_V_DIM), lambda b, h: (b, h)),
        out_shape=jax.ShapeDtypeStruct((bsz * seq, nh * ATT_V_DIM), BF16),
        scratch_shapes=[
            pltpu.VMEM((seq, 2 * ATT_V_DIM), BF16),
            pltpu.VMEM((ATT_V_DIM + ATT_SUM_ROWS, seq), BF16),
        ] + [pltpu.VMEM(((c + 1) * blk, 2 * blk), F32) for c in range(n_blk)]
          + [pltpu.VMEM(((c + 1) * blk, 2 * blk), BF16) for c in range(n_blk)],
        compiler_params=pltpu.CompilerParams(
            dimension_semantics=("arbitrary", "arbitrary"), vmem_limit_bytes=VMEM_LIMIT),
        name="diff_attention",
    )(slopes, jnp.zeros((1,), jnp.int32), lq1, lk1, lq2, lk2, subln_g, qkv, qkv, qkv)


def _s5_disc_kernel(lre_ref, lim_ref, ldt_ref, bre_ref, bim_ref, are_ref, aim_ref, bbre_ref, bbim_ref):
    dt = jnp.exp(ldt_ref[...])
    lre = jnp.minimum(lre_ref[...], -1e-4)
    lim = lim_ref[...]
    mag = jnp.exp(lre * dt)
    lbar_re = mag * jnp.cos(lim * dt)
    lbar_im = mag * jnp.sin(lim * dt)
    num_re = lbar_re - 1.0
    den = lre * lre + lim * lim
    coef_re = (num_re * lre + lbar_im * lim) / den
    coef_im = (lbar_im * lre - num_re * lim) / den
    bre = bre_ref[...]
    bim = bim_ref[...]
    are_ref[...] = lbar_re
    aim_ref[...] = lbar_im
    bbre_ref[...] = coef_re * bre - coef_im * bim
    bbim_ref[...] = coef_re * bim + coef_im * bre


def _s5_discretise(lam_re, lam_im, log_dt, b_re, b_im):
    g, p = lam_re.shape
    hg = b_re.shape[-1]
    n = g * p
    row = lambda a: a.reshape(1, n)
    tr = lambda a: a.reshape(n, hg).T
    ldt = jnp.broadcast_to(log_dt[:, None], (g, p))
    return pl.pallas_call(
        _s5_disc_kernel,
        out_shape=[
            jax.ShapeDtypeStruct((1, n), F32), jax.ShapeDtypeStruct((1, n), F32),
            jax.ShapeDtypeStruct((hg, n), F32), jax.ShapeDtypeStruct((hg, n), F32),
        ],
        name="s5_discretise",
    )(row(lam_re), row(lam_im), row(ldt), tr(b_re), tr(b_im))


def _s5_kernel(u_ref, perm_ref, are_ref, aim_ref, wb_ref, wc_ref, d_ref, wglu_ref, bglu_ref, o_ref,
               x_scr, st_scr, *, bsz, steps, n_chunks):
    @pl.when(pl.program_id(0) == 0)
    def _():
        st_scr[...] = jnp.zeros_like(st_scr)

    cs = CHUNK_STATE
    m = bsz * steps
    perm = perm_ref[...]
    u = jnp.dot(perm, u_ref[...].reshape(m, u_ref.shape[2]), preferred_element_type=F32).astype(BF16)
    def project_in(c):
        x_scr[:, 2 * cs * c:2 * cs * (c + 1)] = jnp.dot(
            u[:, CHUNK_IN * c:CHUNK_IN * (c + 1)], wb_ref[c], preferred_element_type=F32)

    def scan(c):
        re0 = 2 * cs * c
        im0 = re0 + cs
        a_re = jnp.broadcast_to(are_ref[:, cs * c:cs * (c + 1)], (bsz, cs))
        a_im = jnp.broadcast_to(aim_ref[:, cs * c:cs * (c + 1)], (bsz, cs))

        def step(t, carry):
            xr, xi = carry
            r0 = pl.multiple_of(t * bsz, bsz)
            nr = a_re * xr - a_im * xi + x_scr[pl.ds(r0, bsz), re0:re0 + cs]
            ni = a_re * xi + a_im * xr + x_scr[pl.ds(r0, bsz), im0:im0 + cs]
            x_scr[pl.ds(r0, bsz), re0:re0 + cs] = nr
            x_scr[pl.ds(r0, bsz), im0:im0 + cs] = ni
            return nr, ni

        xr, xi = lax.fori_loop(0, steps, step,
                               (st_scr[:, re0:re0 + cs], st_scr[:, im0:im0 + cs]), unroll=True)
        st_scr[:, re0:re0 + cs] = xr
        st_scr[:, im0:im0 + cs] = xi

    def project_out(c):
        return jnp.dot(x_scr[:, 2 * cs * c:2 * cs * (c + 1)].astype(BF16), wc_ref[c],
                       preferred_element_type=F32)

    ys = [None] * n_chunks
    project_in(0)
    for c in range(n_chunks):
        if c + 1 < n_chunks:
            project_in(c + 1)
        if c >= 1:
            ys[c - 1] = project_out(c - 1)
        scan(c)
    ys[n_chunks - 1] = project_out(n_chunks - 1)
    y = jnp.concatenate(ys, axis=1) + d_ref[...] * u.astype(F32)
    y = jax.nn.gelu(y)
    z = jnp.dot(y.astype(BF16), wglu_ref[...], preferred_element_type=F32) + bglu_ref[...]
    out = (y * _sigmoid(z)).astype(BF16)
    out = lax.dot_general(perm, out, (((0,), (0,)), ((), ())), preferred_element_type=F32)
    o_ref[...] = out.astype(o_ref.dtype).reshape(o_ref.shape)


def _s5(u_bt, lbar_re, lbar_im, wb, wc, d_row, w_glu, b_glu, bsz, seq):
    ssm_w = u_bt.shape[2]
    n_chunks = wb.shape[0]
    steps = S5_STEPS
    m = steps * bsz
    n_state = n_chunks * CHUNK_STATE
    kern = functools.partial(_s5_kernel, bsz=bsz, steps=steps, n_chunks=n_chunks)
    const2 = lambda i: (0, 0)
    const3 = lambda i: (0, 0, 0)
    r = jnp.arange(m)
    perm = (r[None, :] == ((r % bsz) * steps + r // bsz)[:, None]).astype(BF16)
    return pl.pallas_call(
        kern,
        grid=(seq // steps,),
        in_specs=[
            pl.BlockSpec((bsz, steps, ssm_w), lambda i: (0, i, 0)),
            pl.BlockSpec((m, m), const2),
            pl.BlockSpec((1, n_state), const2),
            pl.BlockSpec((1, n_state), const2),
            pl.BlockSpec(wb.shape, const3),
            pl.BlockSpec(wc.shape, const3),
            pl.BlockSpec((1, ssm_w), const2),
            pl.BlockSpec(w_glu.shape, const2),
            pl.BlockSpec((1, ssm_w), const2),
        ],
        out_specs=pl.BlockSpec((bsz, steps, ssm_w), lambda i: (0, i, 0)),
        out_shape=jax.ShapeDtypeStruct((bsz, seq, ssm_w), BF16),
        scratch_shapes=[
            pltpu.VMEM((m, 2 * n_state), F32),
            pltpu.VMEM((bsz, 2 * n_state), F32),
        ],
        compiler_params=pltpu.CompilerParams(
            dimension_semantics=("arbitrary",), vmem_limit_bytes=VMEM_LIMIT),
        name="s5_scan",
    )(u_bt, perm, lbar_re, lbar_im, wb, wc, d_row, w_glu, b_glu)


def _s5_block_diag_weights(bbar_re, bbar_im, c_re, c_im):
    hg, n = bbar_re.shape
    g = n // SSM_STATE
    nc = g // GROUPS_PER_CHUNK
    eye = jnp.eye(GROUPS_PER_CHUNK, dtype=F32)

    def wb_part(bb):
        t = bb.reshape(hg, nc, GROUPS_PER_CHUNK, SSM_STATE)
        w = jnp.einsum('ab,hcbp->cahbp', eye, t)
        return w.reshape(nc, GROUPS_PER_CHUNK * hg, GROUPS_PER_CHUNK * SSM_STATE)

    def wc_part(cc):
        t = cc.reshape(nc, GROUPS_PER_CHUNK, hg, SSM_STATE)
        w = jnp.einsum('ab,cbhp->capbh', eye, t)
        return w.reshape(nc, GROUPS_PER_CHUNK * SSM_STATE, GROUPS_PER_CHUNK * hg)

    wb = jnp.concatenate([wb_part(bbar_re), wb_part(bbar_im)], axis=2).astype(BF16)
    wc = jnp.concatenate([wc_part(c_re), -wc_part(c_im)], axis=1).astype(BF16)
    return wb, wc


def _out_kernel(att_ref, za_ref, ssm_ref, rest_ref, x_ref, woa_ref, wos_ref, wout_ref, fg_ref, o_ref,
                *, ssm_w, d_model):
    ya = jnp.dot(att_ref[...] * za_ref[...], woa_ref[...], preferred_element_type=F32)
    ys = jnp.dot(ssm_ref[...] * rest_ref[:, 0:ssm_w], wos_ref[...], preferred_element_type=F32)
    ga = rest_ref[:, ssm_w:ssm_w + d_model].astype(F32)
    gs = rest_ref[:, ssm_w + d_model:ssm_w + 2 * d_model].astype(F32)
    merged = ga * ya + gs * ys
    o = x_ref[...] + jnp.dot(merged.astype(BF16), wout_ref[...], preferred_element_type=F32)
    ms = jnp.mean(o * o, axis=-1, keepdims=True)
    o_ref[...] = o * lax.rsqrt(ms + EPS) * fg_ref[...]


def _out_proj(att, za, ssm, rest, x2, w_o_att, w_o_ssm, w_out, final_g, bsz, seq):
    d_model = x2.shape[1]
    att_w = att.shape[1]
    ssm_w = w_o_ssm.shape[0]
    tm = ROW_TILE
    nl = seq // tm
    row = lambda b, l: (b * nl + l, 0)
    const = lambda b, l: (0, 0)
    kern = functools.partial(_out_kernel, ssm_w=ssm_w, d_model=d_model)
    return pl.pallas_call(
        kern,
        grid=(bsz, nl),
        in_specs=[
            pl.BlockSpec((tm, att_w), row),
            pl.BlockSpec((tm, att_w), row),
            pl.BlockSpec((tm, ssm_w), row),
            pl.BlockSpec((tm, rest.shape[1]), row),
            pl.BlockSpec((tm, d_model), row),
            pl.BlockSpec(w_o_att.shape, const),
            pl.BlockSpec(w_o_ssm.shape, const),
            pl.BlockSpec(w_out.shape, const),
            pl.BlockSpec((1, d_model), const),
        ],
        out_specs=pl.BlockSpec((tm, d_model), row),
        out_shape=jax.ShapeDtypeStruct((bsz * seq, d_model), F32),
        compiler_params=pltpu.CompilerParams(
            dimension_semantics=("arbitrary", "arbitrary"), vmem_limit_bytes=VMEM_LIMIT),
        name="out_proj",
    )(att, za, ssm, rest, x2, w_o_att, w_o_ssm, w_out, final_g)


def kernel(x, norm_g, w_in, lambda_q1, lambda_k1, lambda_q2, lambda_k2, subln_g, w_o_att,
           ssm_lambda_re, ssm_lambda_im, ssm_log_dt, ssm_b_re, ssm_b_im, ssm_c_re, ssm_c_im,
           ssm_d, w_glu, b_glu, w_o_ssm, w_out, final_g):
    bsz, seq, d_model = x.shape
    depth = w_in.shape[0]
    assert depth == 1, "single-layer block"
    att_w = w_o_att.shape[1]
    ssm_w = w_o_ssm.shape[1]
    assert att_w == ATT_HEADS * ATT_V_DIM
    assert w_in.shape[2] == 4 * att_w + 2 * ssm_w + 2 * d_model
    assert seq % ROW_TILE == 0 and seq % ATT_BLOCK == 0 and seq % S5_STEPS == 0
    assert bsz % 16 == 0, "batch rows fill whole bf16 sublane tiles in the S5 kernel"
    assert (ssm_w // SSM_GROUP) % GROUPS_PER_CHUNK == 0

    x2 = x.reshape(bsz * seq, d_model)
    qkv, za, u, rest = _in_proj(x2, norm_g[0][None], w_in[0].astype(BF16), bsz, seq, att_w, ssm_w)

    slopes = jnp.asarray([2.0 ** (-8.0 * (h + 1) / ATT_HEADS) for h in range(ATT_HEADS)], F32)
    att = _attention(qkv, slopes, lambda_q1, lambda_k1, lambda_q2, lambda_k2, subln_g, bsz, seq)

    lbar_re, lbar_im, bbar_re, bbar_im = _s5_discretise(
        ssm_lambda_re[0], ssm_lambda_im[0], ssm_log_dt[0], ssm_b_re[0], ssm_b_im[0])
    wb, wc = _s5_block_diag_weights(bbar_re, bbar_im, ssm_c_re[0], ssm_c_im[0])
    ssm = _s5(u.reshape(bsz, seq, ssm_w), lbar_re, lbar_im, wb, wc,
              ssm_d[0].reshape(1, ssm_w), w_glu[0].astype(BF16), b_glu[0][None], bsz, seq)

    out = _out_proj(att, za, ssm.reshape(bsz * seq, ssm_w), rest, x2,
                    w_o_att[0].astype(BF16), w_o_ssm[0].astype(BF16), w_out[0].astype(BF16),
                    final_g[None], bsz, seq)
    return out.reshape(bsz, seq, d_model)
```

```python
import functools
import math

import jax
import jax.numpy as jnp
from jax import lax
from jax.experimental import pallas as pl
from jax.experimental.pallas import tpu as pltpu

EPS = 1e-5
ATT_HEADS = 8
ATT_HEAD_DIM = 64
ATT_V_DIM = 2 * ATT_HEAD_DIM
SSM_GROUP = 16
SSM_STATE = 64
LAMBDA_INIT = 0.8 - 0.6 * math.exp(-0.3 * 0)
LOG2E = math.log2(math.e)
Q_PRESCALE = ATT_HEAD_DIM ** -0.5 * LOG2E
BIAS_TERMS = 3

GROUPS_PER_CHUNK = 8
CHUNK_IN = GROUPS_PER_CHUNK * SSM_GROUP
CHUNK_STATE = GROUPS_PER_CHUNK * SSM_STATE

ROW_TILE = 1024
ATT_BLOCK = 256
ATT_SUM_ROWS = 16
S5_STEPS = 64
S5_PERM_STEPS = 32
VMEM_LIMIT = 56 * 1024 * 1024

F32 = jnp.float32
BF16 = jnp.bfloat16


def _sigmoid(x):
    return 0.5 * jnp.tanh(0.5 * x) + 0.5


def _silu(x):
    return x * _sigmoid(x)


def _in_proj_kernel(x_ref, g_ref, w_ref, qkv_ref, za_ref, u_ref, zs_ref, gate_ref, *, att_w, ssm_w, d_model):
    x = x_ref[...]
    ms = jnp.mean(x * x, axis=-1, keepdims=True)
    h = (x * lax.rsqrt(ms + EPS) * g_ref[...]).astype(BF16)

    def proj(dst_ref, dst0, src0, width, fn=None, step=512):
        for o in range(0, width, step):
            r = jnp.dot(h, w_ref[:, src0 + o:src0 + o + step], preferred_element_type=F32)
            if fn is not None:
                r = fn(r)
            dst_ref[:, dst0 + o:dst0 + o + step] = r.astype(dst_ref.dtype)

    q0, z0 = 0, 3 * att_w
    u0 = z0 + att_w
    r0 = u0 + ssm_w
    proj(za_ref, 0, z0, att_w, fn=_silu)
    proj(zs_ref, 0, r0, ssm_w, fn=_silu)
    proj(gate_ref, 0, r0 + ssm_w, 2 * d_model, fn=_sigmoid)
    proj(qkv_ref, 0, q0, att_w, fn=lambda r: r * Q_PRESCALE)
    proj(u_ref, 0, u0, ssm_w)
    proj(qkv_ref, att_w, q0 + att_w, 2 * att_w)


def _in_proj(x2, norm_g, w_in, bsz, seq, att_w, ssm_w):
    d_model = x2.shape[1]
    tm = ROW_TILE
    nl = seq // tm
    kern = functools.partial(_in_proj_kernel, att_w=att_w, ssm_w=ssm_w, d_model=d_model)
    row = lambda b, l: (b * nl + l, 0)
    return pl.pallas_call(
        kern,
        grid=(bsz, nl),
        in_specs=[
            pl.BlockSpec((tm, d_model), row),
            pl.BlockSpec((1, d_model), lambda b, l: (0, 0)),
            pl.BlockSpec(w_in.shape, lambda b, l: (0, 0), pipeline_mode=pl.Buffered(1)),
        ],
        out_specs=[
            pl.BlockSpec((tm, 3 * att_w), row),
            pl.BlockSpec((tm, att_w), row),
            pl.BlockSpec((tm, ssm_w), row),
            pl.BlockSpec((tm, ssm_w), row),
            pl.BlockSpec((tm, 2 * d_model), row),
        ],
        out_shape=[
            jax.ShapeDtypeStruct((bsz * seq, 3 * att_w), BF16),
            jax.ShapeDtypeStruct((bsz * seq, att_w), BF16),
            jax.ShapeDtypeStruct((bsz * seq, ssm_w), BF16),
            jax.ShapeDtypeStruct((bsz * seq, ssm_w), BF16),
            jax.ShapeDtypeStruct((bsz * seq, 2 * d_model), BF16),
        ],
        compiler_params=pltpu.CompilerParams(
            dimension_semantics=("arbitrary", "arbitrary"), vmem_limit_bytes=VMEM_LIMIT),
        name="in_proj",
    )(x2, norm_g, w_in)


def _split_bf16(v, n):
    parts = []
    for _ in range(n):
        p = v.astype(BF16)
        parts.append(p)
        v = v - p.astype(F32)
    return parts


def _attn_kernel(slopes_ref, zero_ref, lq1_ref, lk1_ref, lq2_ref, lk2_ref, sg_ref, q_ref, k_ref, v_ref,
                 za_ref, o_ref, kx_ref, vt_ref, *sp_refs):
    blk = ATT_BLOCK
    hd = ATT_HEAD_DIM
    vd = ATT_V_DIM
    n_blk = k_ref.shape[0] // blk
    s_refs, p_refs = sp_refs[:n_blk], sp_refs[n_blk:]

    slope2 = slopes_ref[pl.program_id(1)] * LOG2E
    lane = lax.broadcasted_iota(jnp.int32, (blk, vd), 1)
    pos = lax.broadcasted_iota(jnp.int32, (blk, vd), 0)
    local = jnp.zeros((blk, vd), F32)
    for t, p in enumerate(_split_bf16(slope2 * pos.astype(F32), BIAS_TERMS)):
        local = jnp.where(lane == t, p.astype(F32), local)
    lane_row = lax.broadcasted_iota(jnp.int32, (1, vd), 1)
    for j in range(n_blk):
        rows = pl.ds(j * blk, blk)
        offset = jnp.zeros((1, vd), F32)
        for t, p in enumerate(_split_bf16(slope2 * jnp.full((1, vd), j * blk, F32), BIAS_TERMS)):
            offset = jnp.where(lane_row == BIAS_TERMS + t, p.astype(F32), offset)
        kx_ref[rows, 0:vd] = k_ref[rows, :]
        kx_ref[rows, vd:2 * vd] = (local + offset).astype(BF16)
        vt_ref[0:vd, j * blk:(j + 1) * blk] = v_ref[rows, :].astype(F32).T.astype(BF16)
    sub = lax.broadcasted_iota(jnp.int32, (ATT_SUM_ROWS, vt_ref.shape[1]), 0)
    vt_ref[vd:vd + ATT_SUM_ROWS, :] = jnp.where(sub == 0, 1.0, 0.0).astype(BF16)

    e1 = jnp.exp(jnp.sum(lq1_ref[...] * lk1_ref[...], axis=-1, keepdims=True))
    e2 = jnp.exp(jnp.sum(lq2_ref[...] * lk2_ref[...], axis=-1, keepdims=True))
    lam = e1 - e2 + LAMBDA_INIT
    gain = sg_ref[...] * (1.0 - LAMBDA_INIT)
    nt = (((1,), (1,)), ((), ()))

    key = lax.broadcasted_iota(jnp.int32, (blk, 2 * blk), 0)
    qry = lax.broadcasted_iota(jnp.int32, (blk, 2 * blk), 1)
    causal = key <= jnp.where(qry >= blk, qry - blk, qry)

    dyn0 = zero_ref[0]
    order = list(reversed(range(n_blk)))
    col_max = {}
    for c in order:
        nk = (c + 1) * blk
        q = q_ref[c * blk:(c + 1) * blk, :].astype(F32)
        zero = jnp.zeros_like(q)
        ones = jnp.where(lane < 2 * BIAS_TERMS, jnp.ones_like(q), zero)
        qx = jnp.concatenate([
            jnp.concatenate([jnp.where(lane < hd, q, zero), ones], axis=1),
            jnp.concatenate([jnp.where(lane >= hd, q, zero), ones], axis=1)], axis=0).astype(BF16)
        s = lax.dot_general(kx_ref[0:nk, :], qx, nt, preferred_element_type=F32)
        m = None
        for j in range(c + 1):
            sj = s[j * blk:(j + 1) * blk, :]
            if j == c:
                sj = jnp.where(causal, sj, -jnp.inf)
            s_refs[c][j * blk:(j + 1) * blk, :] = sj
            mj = jnp.max(sj, axis=0, keepdims=True)
            m = mj if m is None else jnp.maximum(m, mj)
        col_max[c] = m

    for c in order:
        for j in range(c + 1):
            sj = s_refs[c][pl.ds(pl.multiple_of(dyn0 + j * blk, blk), blk), :]
            p_refs[c][j * blk:(j + 1) * blk, :] = jnp.exp2(sj - col_max[c]).astype(BF16)

    for c in order:
        nk = (c + 1) * blk
        p = p_refs[c][pl.ds(pl.multiple_of(dyn0, blk), nk), :]
        acc = jnp.dot(vt_ref[:, 0:nk], p, preferred_element_type=F32)
        r = 1.0 / acc[vd:vd + 1, :]
        ot = acc[0:vd, 0:blk] * r[:, 0:blk] - acc[0:vd, blk:2 * blk] * (lam * r[:, blk:2 * blk])
        ms = jnp.mean(ot * ot, axis=0, keepdims=True)
        att = (ot * lax.rsqrt(ms + EPS)).T * gain
        o_ref[c * blk:(c + 1) * blk, :] = att.astype(BF16) * za_ref[c * blk:(c + 1) * blk, :]


def _attention(qkv, za, slopes, lq1, lk1, lq2, lk2, subln_g, bsz, seq):
    blk = ATT_BLOCK
    n_blk = seq // blk
    nh = ATT_HEADS
    vec = pl.BlockSpec((1, ATT_HEAD_DIM), lambda b, h: (0, 0))
    head = lambda first: pl.BlockSpec((seq, ATT_V_DIM), lambda b, h: (b, first + h))
    return pl.pallas_call(
        _attn_kernel,
        grid=(bsz, nh),
        in_specs=[
            pl.BlockSpec(memory_space=pltpu.SMEM),
            pl.BlockSpec(memory_space=pltpu.SMEM),
            vec, vec, vec, vec,
            pl.BlockSpec((1, ATT_V_DIM), lambda b, h: (0, 0)),
            head(0), head(nh), head(2 * nh),
            head(0),
        ],
        out_specs=head(0),
        out_shape=jax.ShapeDtypeStruct((bsz * seq, nh * ATT_V_DIM), BF16),
        scratch_shapes=[
            pltpu.VMEM((seq, 2 * ATT_V_DIM), BF16),
            pltpu.VMEM((ATT_V_DIM + ATT_SUM_ROWS, seq), BF16),
        ] + [pltpu.VMEM(((c + 1) * blk, 2 * blk), F32) for c in range(n_blk)]
          + [pltpu.VMEM(((c + 1) * blk, 2 * blk), BF16) for c in range(n_blk)],
        compiler_params=pltpu.CompilerParams(
            dimension_semantics=("arbitrary", "arbitrary"), vmem_limit_bytes=VMEM_LIMIT),
        name="diff_attention",
    )(slopes, jnp.zeros((1,), jnp.int32), lq1, lk1, lq2, lk2, subln_g, qkv, qkv, qkv, za)


def _s5_disc_kernel(lre_ref, lim_ref, ldt_ref, bre_ref, bim_ref, are_ref, aim_ref, bbre_ref, bbim_ref):
    dt = jnp.exp(ldt_ref[...])
    lre = jnp.minimum(lre_ref[...], -1e-4)
    lim = lim_ref[...]
    mag = jnp.exp(lre * dt)
    lbar_re = mag * jnp.cos(lim * dt)
    lbar_im = mag * jnp.sin(lim * dt)
    num_re = lbar_re - 1.0
    den = lre * lre + lim * lim
    coef_re = (num_re * lre + lbar_im * lim) / den
    coef_im = (lbar_im * lre - num_re * lim) / den
    bre = bre_ref[...]
    bim = bim_ref[...]
    are_ref[...] = lbar_re
    aim_ref[...] = lbar_im
    bbre_ref[...] = coef_re * bre - coef_im * bim
    bbim_ref[...] = coef_re * bim + coef_im * bre


def _s5_discretise(lam_re, lam_im, log_dt, b_re, b_im):
    g, p = lam_re.shape
    hg = b_re.shape[-1]
    n = g * p
    row = lambda a: a.reshape(1, n)
    tr = lambda a: a.reshape(n, hg).T
    ldt = jnp.broadcast_to(log_dt[:, None], (g, p))
    return pl.pallas_call(
        _s5_disc_kernel,
        out_shape=[
            jax.ShapeDtypeStruct((1, n), F32), jax.ShapeDtypeStruct((1, n), F32),
            jax.ShapeDtypeStruct((hg, n), F32), jax.ShapeDtypeStruct((hg, n), F32),
        ],
        name="s5_discretise",
    )(row(lam_re), row(lam_im), row(ldt), tr(b_re), tr(b_im))


def _s5_kernel(u_ref, zs_ref, perm_ref, are_ref, aim_ref, wb_ref, wc_ref, d_ref, wglu_ref, bglu_ref, o_ref,
               x_scr, st_scr, *, bsz, steps, n_chunks):
    @pl.when(pl.program_id(0) == 0)
    def _():
        st_scr[...] = jnp.zeros_like(st_scr)

    cs = CHUNK_STATE
    perm = perm_ref[...]
    perm_steps = perm.shape[0] // bsz
    pm = bsz * perm_steps
    u = jnp.concatenate(
        [jnp.dot(perm, u_ref[:, t0:t0 + perm_steps, :].reshape(pm, u_ref.shape[2]),
                 preferred_element_type=F32) for t0 in range(0, steps, perm_steps)], axis=0).astype(BF16)

    def project_in(c):
        x_scr[:, 2 * cs * c:2 * cs * (c + 1)] = jnp.dot(
            u[:, CHUNK_IN * c:CHUNK_IN * (c + 1)], wb_ref[c], preferred_element_type=F32)

    def scan(c):
        re0 = 2 * cs * c
        im0 = re0 + cs
        a_re = jnp.broadcast_to(are_ref[:, cs * c:cs * (c + 1)], (bsz, cs))
        a_im = jnp.broadcast_to(aim_ref[:, cs * c:cs * (c + 1)], (bsz, cs))

        def step(t, carry):
            xr, xi = carry
            r0 = pl.multiple_of(t * bsz, bsz)
            nr = a_re * xr - a_im * xi + x_scr[pl.ds(r0, bsz), re0:re0 + cs]
            ni = a_re * xi + a_im * xr + x_scr[pl.ds(r0, bsz), im0:im0 + cs]
            x_scr[pl.ds(r0, bsz), re0:re0 + cs] = nr
            x_scr[pl.ds(r0, bsz), im0:im0 + cs] = ni
            return nr, ni

        xr, xi = lax.fori_loop(0, steps, step,
                               (st_scr[:, re0:re0 + cs], st_scr[:, im0:im0 + cs]), unroll=True)
        st_scr[:, re0:re0 + cs] = xr
        st_scr[:, im0:im0 + cs] = xi

    def project_out(c):
        return jnp.dot(x_scr[:, 2 * cs * c:2 * cs * (c + 1)].astype(BF16), wc_ref[c],
                       preferred_element_type=F32)

    ys = [None] * n_chunks
    project_in(0)
    for c in range(n_chunks):
        if c + 1 < n_chunks:
            project_in(c + 1)
        if c >= 1:
            ys[c - 1] = project_out(c - 1)
        scan(c)
    ys[n_chunks - 1] = project_out(n_chunks - 1)
    y = jnp.concatenate(ys, axis=1) + d_ref[...] * u.astype(F32)
    y = jax.nn.gelu(y)
    z = jnp.dot(y.astype(BF16), wglu_ref[...], preferred_element_type=F32) + bglu_ref[...]
    out = (y * _sigmoid(z)).astype(BF16)
    for i, t0 in enumerate(range(0, steps, perm_steps)):
        part = lax.dot_general(perm, out[i * pm:(i + 1) * pm, :], (((0,), (0,)), ((), ())),
                               preferred_element_type=F32)
        o_ref[:, t0:t0 + perm_steps, :] = (part.astype(BF16).reshape(bsz, perm_steps, part.shape[1])
                                           * zs_ref[:, t0:t0 + perm_steps, :])


def _s5(u_bt, zs_bt, lbar_re, lbar_im, wb, wc, d_row, w_glu, b_glu, bsz, seq):
    ssm_w = u_bt.shape[2]
    n_chunks = wb.shape[0]
    steps = S5_STEPS
    m = steps * bsz
    n_state = n_chunks * CHUNK_STATE
    kern = functools.partial(_s5_kernel, bsz=bsz, steps=steps, n_chunks=n_chunks)
    const2 = lambda i: (0, 0)
    const3 = lambda i: (0, 0, 0)
    rows = pl.BlockSpec((bsz, steps, ssm_w), lambda i: (0, i, 0))
    pm = bsz * S5_PERM_STEPS
    r = jnp.arange(pm)
    perm = (r[None, :] == ((r % bsz) * S5_PERM_STEPS + r // bsz)[:, None]).astype(BF16)
    return pl.pallas_call(
        kern,
        grid=(seq // steps,),
        in_specs=[
            rows,
            rows,
            pl.BlockSpec((pm, pm), const2),
            pl.BlockSpec((1, n_state), const2),
            pl.BlockSpec((1, n_state), const2),
            pl.BlockSpec(wb.shape, const3),
            pl.BlockSpec(wc.shape, const3),
            pl.BlockSpec((1, ssm_w), const2),
            pl.BlockSpec(w_glu.shape, const2),
            pl.BlockSpec((1, ssm_w), const2),
        ],
        out_specs=rows,
        out_shape=jax.ShapeDtypeStruct((bsz, seq, ssm_w), BF16),
        scratch_shapes=[
            pltpu.VMEM((m, 2 * n_state), F32),
            pltpu.VMEM((bsz, 2 * n_state), F32),
        ],
        compiler_params=pltpu.CompilerParams(
            dimension_semantics=("arbitrary",), vmem_limit_bytes=VMEM_LIMIT),
        name="s5_scan",
    )(u_bt, zs_bt, perm, lbar_re, lbar_im, wb, wc, d_row, w_glu, b_glu)


def _s5_block_diag_weights(bbar_re, bbar_im, c_re, c_im):
    hg, n = bbar_re.shape
    g = n // SSM_STATE
    nc = g // GROUPS_PER_CHUNK
    eye = jnp.eye(GROUPS_PER_CHUNK, dtype=F32)

    def wb_part(bb):
        t = bb.reshape(hg, nc, GROUPS_PER_CHUNK, SSM_STATE)
        w = jnp.einsum('ab,hcbp->cahbp', eye, t)
        return w.reshape(nc, GROUPS_PER_CHUNK * hg, GROUPS_PER_CHUNK * SSM_STATE)

    def wc_part(cc):
        t = cc.reshape(nc, GROUPS_PER_CHUNK, hg, SSM_STATE)
        w = jnp.einsum('ab,cbhp->capbh', eye, t)
        return w.reshape(nc, GROUPS_PER_CHUNK * SSM_STATE, GROUPS_PER_CHUNK * hg)

    wb = jnp.concatenate([wb_part(bbar_re), wb_part(bbar_im)], axis=2).astype(BF16)
    wc = jnp.concatenate([wc_part(c_re), -wc_part(c_im)], axis=1).astype(BF16)
    return wb, wc


def _out_kernel(att_ref, ssm_ref, gate_ref, x_ref, woa_ref, wos_ref, wout_ref, fg_ref, o_ref, *, d_model):
    ya = jnp.dot(att_ref[...], woa_ref[...], preferred_element_type=F32)
    ys = jnp.dot(ssm_ref[...], wos_ref[...], preferred_element_type=F32)
    ga = gate_ref[:, 0:d_model].astype(F32)
    gs = gate_ref[:, d_model:2 * d_model].astype(F32)
    merged = ga * ya + gs * ys
    o = x_ref[...] + jnp.dot(merged.astype(BF16), wout_ref[...], preferred_element_type=F32)
    ms = jnp.mean(o * o, axis=-1, keepdims=True)
    o_ref[...] = o * lax.rsqrt(ms + EPS) * fg_ref[...]


def _out_proj(att, ssm, gate, x2, w_o_att, w_o_ssm, w_out, final_g, bsz, seq):
    d_model = x2.shape[1]
    att_w = att.shape[1]
    ssm_w = w_o_ssm.shape[0]
    tm = ROW_TILE
    nl = seq // tm
    row = lambda b, l: (b * nl + l, 0)
    const = lambda b, l: (0, 0)
    kern = functools.partial(_out_kernel, d_model=d_model)
    return pl.pallas_call(
        kern,
        grid=(bsz, nl),
        in_specs=[
            pl.BlockSpec((tm, att_w), row),
            pl.BlockSpec((tm, ssm_w), row),
            pl.BlockSpec((tm, 2 * d_model), row),
            pl.BlockSpec((tm, d_model), row),
            pl.BlockSpec(w_o_att.shape, const),
            pl.BlockSpec(w_o_ssm.shape, const),
            pl.BlockSpec(w_out.shape, const),
            pl.BlockSpec((1, d_model), const),
        ],
        out_specs=pl.BlockSpec((tm, d_model), row),
        out_shape=jax.ShapeDtypeStruct((bsz * seq, d_model), F32),
        compiler_params=pltpu.CompilerParams(
            dimension_semantics=("arbitrary", "arbitrary"), vmem_limit_bytes=VMEM_LIMIT),
        name="out_proj",
    )(att, ssm, gate, x2, w_o_att, w_o_ssm, w_out, final_g)


def kernel(x, norm_g, w_in, lambda_q1, lambda_k1, lambda_q2, lambda_k2, subln_g, w_o_att,
           ssm_lambda_re, ssm_lambda_im, ssm_log_dt, ssm_b_re, ssm_b_im, ssm_c_re, ssm_c_im,
           ssm_d, w_glu, b_glu, w_o_ssm, w_out, final_g):
    bsz, seq, d_model = x.shape
    depth = w_in.shape[0]
    assert depth == 1, "single-layer block"
    att_w = w_o_att.shape[1]
    ssm_w = w_o_ssm.shape[1]
    assert att_w == ATT_HEADS * ATT_V_DIM
    assert w_in.shape[2] == 4 * att_w + 2 * ssm_w + 2 * d_model
    assert seq % ROW_TILE == 0 and seq % ATT_BLOCK == 0 and seq % S5_STEPS == 0
    assert S5_STEPS % S5_PERM_STEPS == 0
    assert bsz % 16 == 0, "batch rows fill whole bf16 sublane tiles in the S5 kernel"
    assert (ssm_w // SSM_GROUP) % GROUPS_PER_CHUNK == 0

    x2 = x.reshape(bsz * seq, d_model)
    qkv, za, u, zs, gate = _in_proj(x2, norm_g[0][None], w_in[0].astype(BF16), bsz, seq, att_w, ssm_w)

    slopes = jnp.asarray([2.0 ** (-8.0 * (h + 1) / ATT_HEADS) for h in range(ATT_HEADS)], F32)
    att = _attention(qkv, za, slopes, lambda_q1, lambda_k1, lambda_q2, lambda_k2, subln_g, bsz, seq)

    lbar_re, lbar_im, bbar_re, bbar_im = _s5_discretise(
        ssm_lambda_re[0], ssm_lambda_im[0], ssm_log_dt[0], ssm_b_re[0], ssm_b_im[0])
    wb, wc = _s5_block_diag_weights(bbar_re, bbar_im, ssm_c_re[0], ssm_c_im[0])
    ssm = _s5(u.reshape(bsz, seq, ssm_w), zs.reshape(bsz, seq, ssm_w), lbar_re, lbar_im, wb, wc,
              ssm_d[0].reshape(1, ssm_w), w_glu[0].astype(BF16), b_glu[0][None], bsz, seq)

    out = _out_proj(att, ssm.reshape(bsz * seq, ssm_w), gate, x2,
                    w_o_att[0].astype(BF16), w_o_ssm[0].astype(BF16), w_out[0].astype(BF16),
                    final_g[None], bsz, seq)
    return out.reshape(bsz, seq, d_model)
```

```python
import functools
import math

import jax
import jax.numpy as jnp
from jax import lax
from jax.experimental import pallas as pl
from jax.experimental.pallas import tpu as pltpu

EPS = 1e-5
ATT_HEADS = 8
ATT_HEAD_DIM = 64
ATT_V_DIM = 2 * ATT_HEAD_DIM
SSM_GROUP = 16
SSM_STATE = 64
LAMBDA_INIT = 0.8 - 0.6 * math.exp(-0.3 * 0)
LOG2E = math.log2(math.e)
Q_PRESCALE = ATT_HEAD_DIM ** -0.5 * LOG2E
BIAS_TERMS = 3

GROUPS_PER_CHUNK = 8
CHUNK_IN = GROUPS_PER_CHUNK * SSM_GROUP
CHUNK_STATE = GROUPS_PER_CHUNK * SSM_STATE

ROW_TILE = 1024
ATT_BLOCK = 128
ATT_SUM_ROWS = 16
S5_STEPS = 64
S5_PERM_STEPS = 32
VMEM_LIMIT = 56 * 1024 * 1024

F32 = jnp.float32
BF16 = jnp.bfloat16


def _sigmoid(x):
    return 0.5 * jnp.tanh(0.5 * x) + 0.5


def _silu(x):
    return x * _sigmoid(x)


def _in_proj_kernel(x_ref, g_ref, w_ref, qkv_ref, za_ref, u_ref, zs_ref, gate_ref, *, att_w, ssm_w, d_model):
    x = x_ref[...]
    ms = jnp.mean(x * x, axis=-1, keepdims=True)
    h = (x * lax.rsqrt(ms + EPS) * g_ref[...]).astype(BF16)

    def proj(dst_ref, dst0, src0, width, fn=None, step=512):
        for o in range(0, width, step):
            r = jnp.dot(h, w_ref[:, src0 + o:src0 + o + step], preferred_element_type=F32)
            if fn is not None:
                r = fn(r)
            dst_ref[:, dst0 + o:dst0 + o + step] = r.astype(dst_ref.dtype)

    q0, z0 = 0, 3 * att_w
    u0 = z0 + att_w
    r0 = u0 + ssm_w
    proj(za_ref, 0, z0, att_w, fn=_silu)
    proj(zs_ref, 0, r0, ssm_w, fn=_silu)
    proj(gate_ref, 0, r0 + ssm_w, 2 * d_model, fn=_sigmoid)
    proj(qkv_ref, 0, q0, att_w, fn=lambda r: r * Q_PRESCALE)
    proj(u_ref, 0, u0, ssm_w)
    proj(qkv_ref, att_w, q0 + att_w, 2 * att_w)


def _in_proj(x2, norm_g, w_in, bsz, seq, att_w, ssm_w):
    d_model = x2.shape[1]
    tm = ROW_TILE
    nl = seq // tm
    kern = functools.partial(_in_proj_kernel, att_w=att_w, ssm_w=ssm_w, d_model=d_model)
    row = lambda b, l: (b * nl + l, 0)
    return pl.pallas_call(
        kern,
        grid=(bsz, nl),
        in_specs=[
            pl.BlockSpec((tm, d_model), row),
            pl.BlockSpec((1, d_model), lambda b, l: (0, 0)),
            pl.BlockSpec(w_in.shape, lambda b, l: (0, 0), pipeline_mode=pl.Buffered(1)),
        ],
        out_specs=[
            pl.BlockSpec((tm, 3 * att_w), row),
            pl.BlockSpec((tm, att_w), row),
            pl.BlockSpec((tm, ssm_w), row),
            pl.BlockSpec((tm, ssm_w), row),
            pl.BlockSpec((tm, 2 * d_model), row),
        ],
        out_shape=[
            jax.ShapeDtypeStruct((bsz * seq, 3 * att_w), BF16),
            jax.ShapeDtypeStruct((bsz * seq, att_w), BF16),
            jax.ShapeDtypeStruct((bsz * seq, ssm_w), BF16),
            jax.ShapeDtypeStruct((bsz * seq, ssm_w), BF16),
            jax.ShapeDtypeStruct((bsz * seq, 2 * d_model), BF16),
        ],
        compiler_params=pltpu.CompilerParams(
            dimension_semantics=("arbitrary", "arbitrary"), vmem_limit_bytes=VMEM_LIMIT),
        name="in_proj",
    )(x2, norm_g, w_in)


def _split_bf16(v, n):
    parts = []
    for _ in range(n):
        p = v.astype(BF16)
        parts.append(p)
        v = v - p.astype(F32)
    return parts


def _attn_kernel(slopes_ref, zero_ref, lq1_ref, lk1_ref, lq2_ref, lk2_ref, sg_ref, q_ref, k_ref, v_ref,
                 za_ref, o_ref, kx_ref, vt_ref, *sp_refs):
    blk = ATT_BLOCK
    hd = ATT_HEAD_DIM
    vd = ATT_V_DIM
    n_blk = k_ref.shape[0] // blk
    s_refs, p_refs = sp_refs[:n_blk], sp_refs[n_blk:]

    slope2 = slopes_ref[pl.program_id(1)] * LOG2E
    lane = lax.broadcasted_iota(jnp.int32, (blk, vd), 1)
    pos = lax.broadcasted_iota(jnp.int32, (blk, vd), 0)
    local = jnp.zeros((blk, vd), F32)
    for t, p in enumerate(_split_bf16(slope2 * pos.astype(F32), BIAS_TERMS)):
        local = jnp.where(lane == t, p.astype(F32), local)
    lane_row = lax.broadcasted_iota(jnp.int32, (1, vd), 1)
    for j in range(n_blk):
        rows = pl.ds(j * blk, blk)
        offset = jnp.zeros((1, vd), F32)
        for t, p in enumerate(_split_bf16(slope2 * jnp.full((1, vd), j * blk, F32), BIAS_TERMS)):
            offset = jnp.where(lane_row == BIAS_TERMS + t, p.astype(F32), offset)
        kx_ref[rows, 0:vd] = k_ref[rows, :]
        kx_ref[rows, vd:2 * vd] = (local + offset).astype(BF16)
        vt_ref[0:vd, j * blk:(j + 1) * blk] = v_ref[rows, :].astype(F32).T.astype(BF16)
    sub = lax.broadcasted_iota(jnp.int32, (ATT_SUM_ROWS, vt_ref.shape[1]), 0)
    vt_ref[vd:vd + ATT_SUM_ROWS, :] = jnp.where(sub == 0, 1.0, 0.0).astype(BF16)

    e1 = jnp.exp(jnp.sum(lq1_ref[...] * lk1_ref[...], axis=-1, keepdims=True))
    e2 = jnp.exp(jnp.sum(lq2_ref[...] * lk2_ref[...], axis=-1, keepdims=True))
    lam = e1 - e2 + LAMBDA_INIT
    gain = sg_ref[...] * (1.0 - LAMBDA_INIT)
    nt = (((1,), (1,)), ((), ()))

    key = lax.broadcasted_iota(jnp.int32, (blk, 2 * blk), 0)
    qry = lax.broadcasted_iota(jnp.int32, (blk, 2 * blk), 1)
    causal = key <= jnp.where(qry >= blk, qry - blk, qry)

    dyn0 = zero_ref[0]
    order = list(reversed(range(n_blk)))
    col_max = {}
    for c in order:
        nk = (c + 1) * blk
        q = q_ref[c * blk:(c + 1) * blk, :].astype(F32)
        zero = jnp.zeros_like(q)
        ones = jnp.where(lane < 2 * BIAS_TERMS, jnp.ones_like(q), zero)
        qx = jnp.concatenate([
            jnp.concatenate([jnp.where(lane < hd, q, zero), ones], axis=1),
            jnp.concatenate([jnp.where(lane >= hd, q, zero), ones], axis=1)], axis=0).astype(BF16)
        s = lax.dot_general(kx_ref[0:nk, :], qx, nt, preferred_element_type=F32)
        m = None
        for j in range(c + 1):
            sj = s[j * blk:(j + 1) * blk, :]
            if j == c:
                sj = jnp.where(causal, sj, -jnp.inf)
            s_refs[c][j * blk:(j + 1) * blk, :] = sj
            mj = jnp.max(sj, axis=0, keepdims=True)
            m = mj if m is None else jnp.maximum(m, mj)
        col_max[c] = m

    for c in order:
        for j in range(c + 1):
            sj = s_refs[c][pl.ds(pl.multiple_of(dyn0 + j * blk, blk), blk), :]
            p_refs[c][j * blk:(j + 1) * blk, :] = jnp.exp2(sj - col_max[c]).astype(BF16)

    for c in order:
        nk = (c + 1) * blk
        p = p_refs[c][pl.ds(pl.multiple_of(dyn0, blk), nk), :]
        acc = jnp.dot(vt_ref[:, 0:nk], p, preferred_element_type=F32)
        r = 1.0 / acc[vd:vd + 1, :]
        ot = acc[0:vd, 0:blk] * r[:, 0:blk] - acc[0:vd, blk:2 * blk] * (lam * r[:, blk:2 * blk])
        ms = jnp.mean(ot * ot, axis=0, keepdims=True)
        att = (ot * lax.rsqrt(ms + EPS)).T * gain
        o_ref[c * blk:(c + 1) * blk, :] = att.astype(BF16) * za_ref[c * blk:(c + 1) * blk, :]


def _attention(qkv, za, slopes, lq1, lk1, lq2, lk2, subln_g, bsz, seq):
    blk = ATT_BLOCK
    n_blk = seq // blk
    nh = ATT_HEADS
    vec = pl.BlockSpec((1, ATT_HEAD_DIM), lambda b, h: (0, 0))
    head = lambda first: pl.BlockSpec((seq, ATT_V_DIM), lambda b, h: (b, first + h))
    return pl.pallas_call(
        _attn_kernel,
        grid=(bsz, nh),
        in_specs=[
            pl.BlockSpec(memory_space=pltpu.SMEM),
            pl.BlockSpec(memory_space=pltpu.SMEM),
            vec, vec, vec, vec,
            pl.BlockSpec((1, ATT_V_DIM), lambda b, h: (0, 0)),
            head(0), head(nh), head(2 * nh),
            head(0),
        ],
        out_specs=head(0),
        out_shape=jax.ShapeDtypeStruct((bsz * seq, nh * ATT_V_DIM), BF16),
        scratch_shapes=[
            pltpu.VMEM((seq, 2 * ATT_V_DIM), BF16),
            pltpu.VMEM((ATT_V_DIM + ATT_SUM_ROWS, seq), BF16),
        ] + [pltpu.VMEM(((c + 1) * blk, 2 * blk), F32) for c in range(n_blk)]
          + [pltpu.VMEM(((c + 1) * blk, 2 * blk), BF16) for c in range(n_blk)],
        compiler_params=pltpu.CompilerParams(
            dimension_semantics=("arbitrary", "arbitrary"), vmem_limit_bytes=VMEM_LIMIT),
        name="diff_attention",
    )(slopes, jnp.zeros((1,), jnp.int32), lq1, lk1, lq2, lk2, subln_g, qkv, qkv, qkv, za)


def _s5_disc_kernel(lre_ref, lim_ref, ldt_ref, bre_ref, bim_ref, are_ref, aim_ref, bbre_ref, bbim_ref):
    dt = jnp.exp(ldt_ref[...])
    lre = jnp.minimum(lre_ref[...], -1e-4)
    lim = lim_ref[...]
    mag = jnp.exp(lre * dt)
    lbar_re = mag * jnp.cos(lim * dt)
    lbar_im = mag * jnp.sin(lim * dt)
    num_re = lbar_re - 1.0
    den = lre * lre + lim * lim
    coef_re = (num_re * lre + lbar_im * lim) / den
    coef_im = (lbar_im * lre - num_re * lim) / den
    bre = bre_ref[...]
    bim = bim_ref[...]
    are_ref[...] = lbar_re
    aim_ref[...] = lbar_im
    bbre_ref[...] = coef_re * bre - coef_im * bim
    bbim_ref[...] = coef_re * bim + coef_im * bre


def _s5_discretise(lam_re, lam_im, log_dt, b_re, b_im):
    g, p = lam_re.shape
    hg = b_re.shape[-1]
    n = g * p
    row = lambda a: a.reshape(1, n)
    tr = lambda a: a.reshape(n, hg).T
    ldt = jnp.broadcast_to(log_dt[:, None], (g, p))
    return pl.pallas_call(
        _s5_disc_kernel,
        out_shape=[
            jax.ShapeDtypeStruct((1, n), F32), jax.ShapeDtypeStruct((1, n), F32),
            jax.ShapeDtypeStruct((hg, n), F32), jax.ShapeDtypeStruct((hg, n), F32),
        ],
        name="s5_discretise",
    )(row(lam_re), row(lam_im), row(ldt), tr(b_re), tr(b_im))


def _s5_kernel(u_ref, zs_ref, perm_ref, are_ref, aim_ref, wb_ref, wc_ref, d_ref, wglu_ref, bglu_ref, o_ref,
               x_scr, st_scr, *, bsz, steps, n_chunks):
    @pl.when(pl.program_id(0) == 0)
    def _():
        st_scr[...] = jnp.zeros_like(st_scr)

    cs = CHUNK_STATE
    perm = perm_ref[...]
    perm_steps = perm.shape[0] // bsz
    pm = bsz * perm_steps
    u = jnp.concatenate(
        [jnp.dot(perm, u_ref[:, t0:t0 + perm_steps, :].reshape(pm, u_ref.shape[2]),
                 preferred_element_type=F32) for t0 in range(0, steps, perm_steps)], axis=0).astype(BF16)

    def project_in(c):
        x_scr[:, 2 * cs * c:2 * cs * (c + 1)] = jnp.dot(
            u[:, CHUNK_IN * c:CHUNK_IN * (c + 1)], wb_ref[c], preferred_element_type=F32)

    def scan(c):
        re0 = 2 * cs * c
        im0 = re0 + cs
        a_re = jnp.broadcast_to(are_ref[:, cs * c:cs * (c + 1)], (bsz, cs))
        a_im = jnp.broadcast_to(aim_ref[:, cs * c:cs * (c + 1)], (bsz, cs))

        def step(t, carry):
            xr, xi = carry
            r0 = pl.multiple_of(t * bsz, bsz)
            nr = a_re * xr - a_im * xi + x_scr[pl.ds(r0, bsz), re0:re0 + cs]
            ni = a_re * xi + a_im * xr + x_scr[pl.ds(r0, bsz), im0:im0 + cs]
            x_scr[pl.ds(r0, bsz), re0:re0 + cs] = nr
            x_scr[pl.ds(r0, bsz), im0:im0 + cs] = ni
            return nr, ni

        xr, xi = lax.fori_loop(0, steps, step,
                               (st_scr[:, re0:re0 + cs], st_scr[:, im0:im0 + cs]), unroll=True)
        st_scr[:, re0:re0 + cs] = xr
        st_scr[:, im0:im0 + cs] = xi

    def project_out(c):
        return jnp.dot(x_scr[:, 2 * cs * c:2 * cs * (c + 1)].astype(BF16), wc_ref[c],
                       preferred_element_type=F32)

    ys = [None] * n_chunks
    project_in(0)
    for c in range(n_chunks):
        if c + 1 < n_chunks:
            project_in(c + 1)
        if c >= 1:
            ys[c - 1] = project_out(c - 1)
        scan(c)
    ys[n_chunks - 1] = project_out(n_chunks - 1)
    y = jnp.concatenate(ys, axis=1) + d_ref[...] * u.astype(F32)
    y = jax.nn.gelu(y)
    z = jnp.dot(y.astype(BF16), wglu_ref[...], preferred_element_type=F32) + bglu_ref[...]
    out = (y * _sigmoid(z)).astype(BF16)
    for i, t0 in enumerate(range(0, steps, perm_steps)):
        part = lax.dot_general(perm, out[i * pm:(i + 1) * pm, :], (((0,), (0,)), ((), ())),
                               preferred_element_type=F32)
        o_ref[:, t0:t0 + perm_steps, :] = (part.astype(BF16).reshape(bsz, perm_steps, part.shape[1])
                                           * zs_ref[:, t0:t0 + perm_steps, :])


def _s5(u_bt, zs_bt, lbar_re, lbar_im, wb, wc, d_row, w_glu, b_glu, bsz, seq):
    ssm_w = u_bt.shape[2]
    n_chunks = wb.shape[0]
    steps = S5_STEPS
    m = steps * bsz
    n_state = n_chunks * CHUNK_STATE
    kern = functools.partial(_s5_kernel, bsz=bsz, steps=steps, n_chunks=n_chunks)
    const2 = lambda i: (0, 0)
    const3 = lambda i: (0, 0, 0)
    rows = pl.BlockSpec((bsz, steps, ssm_w), lambda i: (0, i, 0))
    pm = bsz * S5_PERM_STEPS
    r = jnp.arange(pm)
    perm = (r[None, :] == ((r % bsz) * S5_PERM_STEPS + r // bsz)[:, None]).astype(BF16)
    return pl.pallas_call(
        kern,
        grid=(seq // steps,),
        in_specs=[
            rows,
            rows,
            pl.BlockSpec((pm, pm), const2),
            pl.BlockSpec((1, n_state), const2),
            pl.BlockSpec((1, n_state), const2),
            pl.BlockSpec(wb.shape, const3),
            pl.BlockSpec(wc.shape, const3),
            pl.BlockSpec((1, ssm_w), const2),
            pl.BlockSpec(w_glu.shape, const2),
            pl.BlockSpec((1, ssm_w), const2),
        ],
        out_specs=rows,
        out_shape=jax.ShapeDtypeStruct((bsz, seq, ssm_w), BF16),
        scratch_shapes=[
            pltpu.VMEM((m, 2 * n_state), F32),
            pltpu.VMEM((bsz, 2 * n_state), F32),
        ],
        compiler_params=pltpu.CompilerParams(
            dimension_semantics=("arbitrary",), vmem_limit_bytes=VMEM_LIMIT),
        name="s5_scan",
    )(u_bt, zs_bt, perm, lbar_re, lbar_im, wb, wc, d_row, w_glu, b_glu)


def _s5_block_diag_weights(bbar_re, bbar_im, c_re, c_im):
    hg, n = bbar_re.shape
    g = n // SSM_STATE
    nc = g // GROUPS_PER_CHUNK
    eye = jnp.eye(GROUPS_PER_CHUNK, dtype=F32)

    def wb_part(bb):
        t = bb.reshape(hg, nc, GROUPS_PER_CHUNK, SSM_STATE)
        w = jnp.einsum('ab,hcbp->cahbp', eye, t)
        return w.reshape(nc, GROUPS_PER_CHUNK * hg, GROUPS_PER_CHUNK * SSM_STATE)

    def wc_part(cc):
        t = cc.reshape(nc, GROUPS_PER_CHUNK, hg, SSM_STATE)
        w = jnp.einsum('ab,cbhp->capbh', eye, t)
        return w.reshape(nc, GROUPS_PER_CHUNK * SSM_STATE, GROUPS_PER_CHUNK * hg)

    wb = jnp.concatenate([wb_part(bbar_re), wb_part(bbar_im)], axis=2).astype(BF16)
    wc = jnp.concatenate([wc_part(c_re), -wc_part(c_im)], axis=1).astype(BF16)
    return wb, wc


def _out_kernel(att_ref, ssm_ref, gate_ref, x_ref, woa_ref, wos_ref, wout_ref, fg_ref, o_ref, *, d_model):
    ya = jnp.dot(att_ref[...], woa_ref[...], preferred_element_type=F32)
    ys = jnp.dot(ssm_ref[...], wos_ref[...], preferred_element_type=F32)
    ga = gate_ref[:, 0:d_model].astype(F32)
    gs = gate_ref[:, d_model:2 * d_model].astype(F32)
    merged = ga * ya + gs * ys
    o = x_ref[...] + jnp.dot(merged.astype(BF16), wout_ref[...], preferred_element_type=F32)
    ms = jnp.mean(o * o, axis=-1, keepdims=True)
    o_ref[...] = o * lax.rsqrt(ms + EPS) * fg_ref[...]


def _out_proj(att, ssm, gate, x2, w_o_att, w_o_ssm, w_out, final_g, bsz, seq):
    d_model = x2.shape[1]
    att_w = att.shape[1]
    ssm_w = w_o_ssm.shape[0]
    tm = ROW_TILE
    nl = seq // tm
    row = lambda b, l: (b * nl + l, 0)
    const = lambda b, l: (0, 0)
    kern = functools.partial(_out_kernel, d_model=d_model)
    return pl.pallas_call(
        kern,
        grid=(bsz, nl),
        in_specs=[
            pl.BlockSpec((tm, att_w), row),
            pl.BlockSpec((tm, ssm_w), row),
            pl.BlockSpec((tm, 2 * d_model), row),
            pl.BlockSpec((tm, d_model), row),
            pl.BlockSpec(w_o_att.shape, const),
            pl.BlockSpec(w_o_ssm.shape, const),
            pl.BlockSpec(w_out.shape, const),
            pl.BlockSpec((1, d_model), const),
        ],
        out_specs=pl.BlockSpec((tm, d_model), row),
        out_shape=jax.ShapeDtypeStruct((bsz * seq, d_model), F32),
        compiler_params=pltpu.CompilerParams(
            dimension_semantics=("arbitrary", "arbitrary"), vmem_limit_bytes=VMEM_LIMIT),
        name="out_proj",
    )(att, ssm, gate, x2, w_o_att, w_o_ssm, w_out, final_g)


def kernel(x, norm_g, w_in, lambda_q1, lambda_k1, lambda_q2, lambda_k2, subln_g, w_o_att,
           ssm_lambda_re, ssm_lambda_im, ssm_log_dt, ssm_b_re, ssm_b_im, ssm_c_re, ssm_c_im,
           ssm_d, w_glu, b_glu, w_o_ssm, w_out, final_g):
    bsz, seq, d_model = x.shape
    depth = w_in.shape[0]
    assert depth == 1, "single-layer block"
    att_w = w_o_att.shape[1]
    ssm_w = w_o_ssm.shape[1]
    assert att_w == ATT_HEADS * ATT_V_DIM
    assert w_in.shape[2] == 4 * att_w + 2 * ssm_w + 2 * d_model
    assert seq % ROW_TILE == 0 and seq % ATT_BLOCK == 0 and seq % S5_STEPS == 0
    assert S5_STEPS % S5_PERM_STEPS == 0
    assert bsz % 16 == 0, "batch rows fill whole bf16 sublane tiles in the S5 kernel"
    assert (ssm_w // SSM_GROUP) % GROUPS_PER_CHUNK == 0

    x2 = x.reshape(bsz * seq, d_model)
    qkv, za, u, zs, gate = _in_proj(x2, norm_g[0][None], w_in[0].astype(BF16), bsz, seq, att_w, ssm_w)

    slopes = jnp.asarray([2.0 ** (-8.0 * (h + 1) / ATT_HEADS) for h in range(ATT_HEADS)], F32)
    att = _attention(qkv, za, slopes, lambda_q1, lambda_k1, lambda_q2, lambda_k2, subln_g, bsz, seq)

    lbar_re, lbar_im, bbar_re, bbar_im = _s5_discretise(
        ssm_lambda_re[0], ssm_lambda_im[0], ssm_log_dt[0], ssm_b_re[0], ssm_b_im[0])
    wb, wc = _s5_block_diag_weights(bbar_re, bbar_im, ssm_c_re[0], ssm_c_im[0])
    ssm = _s5(u.reshape(bsz, seq, ssm_w), zs.reshape(bsz, seq, ssm_w), lbar_re, lbar_im, wb, wc,
              ssm_d[0].reshape(1, ssm_w), w_glu[0].astype(BF16), b_glu[0][None], bsz, seq)

    out = _out_proj(att, ssm.reshape(bsz * seq, ssm_w), gate, x2,
                    w_o_att[0].astype(BF16), w_o_ssm[0].astype(BF16), w_out[0].astype(BF16),
                    final_g[None], bsz, seq)
    return out.reshape(bsz, seq, d_model)
```

```python
import functools
import math

import jax
import jax.numpy as jnp
from jax import lax
from jax.experimental import pallas as pl
from jax.experimental.pallas import tpu as pltpu

EPS = 1e-5
ATT_HEADS = 8
ATT_HEAD_DIM = 64
ATT_V_DIM = 2 * ATT_HEAD_DIM
SSM_GROUP = 16
SSM_STATE = 64
LAMBDA_INIT = 0.8 - 0.6 * math.exp(-0.3 * 0)
LOG2E = math.log2(math.e)
Q_PRESCALE = ATT_HEAD_DIM ** -0.5 * LOG2E
BIAS_TERMS = 3

GROUPS_PER_CHUNK = 8
CHUNK_IN = GROUPS_PER_CHUNK * SSM_GROUP
CHUNK_STATE = GROUPS_PER_CHUNK * SSM_STATE

ROW_TILE = 1024
ATT_BLOCK = 128
ATT_SUM_ROWS = 16
ATT_HEADS_PER_STEP = 2
S5_STEPS = 64
S5_PERM_STEPS = 32
VMEM_LIMIT = 56 * 1024 * 1024

F32 = jnp.float32
BF16 = jnp.bfloat16


def _sigmoid(x):
    return 0.5 * jnp.tanh(0.5 * x) + 0.5


def _silu(x):
    return x * _sigmoid(x)


def _in_proj_kernel(x_ref, g_ref, w_ref, qkv_ref, za_ref, u_ref, zs_ref, gate_ref, *, att_w, ssm_w, d_model):
    x = x_ref[...]
    ms = jnp.mean(x * x, axis=-1, keepdims=True)
    h = (x * lax.rsqrt(ms + EPS) * g_ref[...]).astype(BF16)

    def proj(dst_ref, dst0, src0, width, fn=None, step=512):
        for o in range(0, width, step):
            r = jnp.dot(h, w_ref[:, src0 + o:src0 + o + step], preferred_element_type=F32)
            if fn is not None:
                r = fn(r)
            dst_ref[:, dst0 + o:dst0 + o + step] = r.astype(dst_ref.dtype)

    q0, z0 = 0, 3 * att_w
    u0 = z0 + att_w
    r0 = u0 + ssm_w
    proj(za_ref, 0, z0, att_w, fn=_silu)
    proj(zs_ref, 0, r0, ssm_w, fn=_silu)
    proj(gate_ref, 0, r0 + ssm_w, 2 * d_model, fn=_sigmoid)
    proj(qkv_ref, 0, q0, att_w, fn=lambda r: r * Q_PRESCALE)
    proj(u_ref, 0, u0, ssm_w)
    proj(qkv_ref, att_w, q0 + att_w, 2 * att_w)


def _in_proj(x2, norm_g, w_in, bsz, seq, att_w, ssm_w):
    d_model = x2.shape[1]
    tm = ROW_TILE
    nl = seq // tm
    kern = functools.partial(_in_proj_kernel, att_w=att_w, ssm_w=ssm_w, d_model=d_model)
    row = lambda b, l: (b * nl + l, 0)
    return pl.pallas_call(
        kern,
        grid=(bsz, nl),
        in_specs=[
            pl.BlockSpec((tm, d_model), row),
            pl.BlockSpec((1, d_model), lambda b, l: (0, 0)),
            pl.BlockSpec(w_in.shape, lambda b, l: (0, 0), pipeline_mode=pl.Buffered(1)),
        ],
        out_specs=[
            pl.BlockSpec((tm, 3 * att_w), row),
            pl.BlockSpec((tm, att_w), row),
            pl.BlockSpec((tm, ssm_w), row),
            pl.BlockSpec((tm, ssm_w), row),
            pl.BlockSpec((tm, 2 * d_model), row),
        ],
        out_shape=[
            jax.ShapeDtypeStruct((bsz * seq, 3 * att_w), BF16),
            jax.ShapeDtypeStruct((bsz * seq, att_w), BF16),
            jax.ShapeDtypeStruct((bsz * seq, ssm_w), BF16),
            jax.ShapeDtypeStruct((bsz * seq, ssm_w), BF16),
            jax.ShapeDtypeStruct((bsz * seq, 2 * d_model), BF16),
        ],
        compiler_params=pltpu.CompilerParams(
            dimension_semantics=("arbitrary", "arbitrary"), vmem_limit_bytes=VMEM_LIMIT),
        name="in_proj",
    )(x2, norm_g, w_in)


def _split_bf16(v, n):
    parts = []
    for _ in range(n):
        p = v.astype(BF16)
        parts.append(p)
        v = v - p.astype(F32)
    return parts


def _attn_kernel(slopes_ref, zero_ref, lq1_ref, lk1_ref, lq2_ref, lk2_ref, sg_ref, q_ref, k_ref, v_ref,
                 za_ref, o_ref, kx_ref, vt_ref, *sp_refs):
    blk = ATT_BLOCK
    hd = ATT_HEAD_DIM
    vd = ATT_V_DIM
    n_blk = k_ref.shape[0] // blk
    heads = range(ATT_HEADS_PER_STEP)
    s_refs = sp_refs[:n_blk]
    p_refs = [sp_refs[(1 + e) * n_blk:(2 + e) * n_blk] for e in heads]

    lane = lax.broadcasted_iota(jnp.int32, (blk, vd), 1)
    pos = lax.broadcasted_iota(jnp.int32, (blk, vd), 0)
    lane_row = lax.broadcasted_iota(jnp.int32, (1, vd), 1)
    sub = lax.broadcasted_iota(jnp.int32, (ATT_SUM_ROWS, vt_ref.shape[2]), 0)
    for e in heads:
        slope2 = slopes_ref[pl.program_id(1) * ATT_HEADS_PER_STEP + e] * LOG2E
        local = jnp.zeros((blk, vd), F32)
        for t, p in enumerate(_split_bf16(slope2 * pos.astype(F32), BIAS_TERMS)):
            local = jnp.where(lane == t, p.astype(F32), local)
        for j in range(n_blk):
            rows = pl.ds(j * blk, blk)
            offset = jnp.zeros((1, vd), F32)
            for t, p in enumerate(_split_bf16(slope2 * jnp.full((1, vd), j * blk, F32), BIAS_TERMS)):
                offset = jnp.where(lane_row == BIAS_TERMS + t, p.astype(F32), offset)
            kx_ref[e, rows, 0:vd] = k_ref[rows, e * vd:(e + 1) * vd]
            kx_ref[e, rows, vd:2 * vd] = (local + offset).astype(BF16)
            vt_ref[e, 0:vd, j * blk:(j + 1) * blk] = (
                v_ref[rows, e * vd:(e + 1) * vd].astype(F32).T.astype(BF16))
        vt_ref[e, vd:vd + ATT_SUM_ROWS, :] = jnp.where(sub == 0, 1.0, 0.0).astype(BF16)

    e1 = jnp.exp(jnp.sum(lq1_ref[...] * lk1_ref[...], axis=-1, keepdims=True))
    e2 = jnp.exp(jnp.sum(lq2_ref[...] * lk2_ref[...], axis=-1, keepdims=True))
    lam = e1 - e2 + LAMBDA_INIT
    gain = sg_ref[...] * (1.0 - LAMBDA_INIT)
    nt = (((1,), (1,)), ((), ()))

    key = lax.broadcasted_iota(jnp.int32, (blk, 2 * blk), 0)
    qry = lax.broadcasted_iota(jnp.int32, (blk, 2 * blk), 1)
    causal = key <= jnp.where(qry >= blk, qry - blk, qry)

    dyn0 = zero_ref[0]
    order = list(reversed(range(n_blk)))
    col_max = {}

    def score_pass(e):
        for c in order:
            nk = (c + 1) * blk
            q = q_ref[c * blk:(c + 1) * blk, e * vd:(e + 1) * vd].astype(F32)
            zero = jnp.zeros_like(q)
            ones = jnp.where(lane < 2 * BIAS_TERMS, jnp.ones_like(q), zero)
            qx = jnp.concatenate([
                jnp.concatenate([jnp.where(lane < hd, q, zero), ones], axis=1),
                jnp.concatenate([jnp.where(lane >= hd, q, zero), ones], axis=1)], axis=0).astype(BF16)
            s = lax.dot_general(kx_ref[e, 0:nk, :], qx, nt, preferred_element_type=F32)
            m = None
            for j in range(c + 1):
                sj = s[j * blk:(j + 1) * blk, :]
                if j == c:
                    sj = jnp.where(causal, sj, -jnp.inf)
                s_refs[c][j * blk:(j + 1) * blk, :] = sj
                mj = jnp.max(sj, axis=0, keepdims=True)
                m = mj if m is None else jnp.maximum(m, mj)
            col_max[e, c] = m

    def softmax_pass(e):
        for c in order:
            for j in range(c + 1):
                sj = s_refs[c][pl.ds(pl.multiple_of(dyn0 + j * blk, blk), blk), :]
                p_refs[e][c][j * blk:(j + 1) * blk, :] = jnp.exp2(sj - col_max[e, c]).astype(BF16)

    def value_pass(e):
        for c in order:
            nk = (c + 1) * blk
            p = p_refs[e][c][pl.ds(pl.multiple_of(dyn0, blk), nk), :]
            acc = jnp.dot(vt_ref[e, :, 0:nk], p, preferred_element_type=F32)
            r = 1.0 / acc[vd:vd + 1, :]
            ot = acc[0:vd, 0:blk] * r[:, 0:blk] - acc[0:vd, blk:2 * blk] * (lam * r[:, blk:2 * blk])
            ms = jnp.mean(ot * ot, axis=0, keepdims=True)
            att = (ot * lax.rsqrt(ms + EPS)).T * gain
            o_ref[c * blk:(c + 1) * blk, e * vd:(e + 1) * vd] = (
                att.astype(BF16) * za_ref[c * blk:(c + 1) * blk, e * vd:(e + 1) * vd])

    score_pass(0)
    softmax_pass(0)
    for e in heads[1:]:
        score_pass(e)
        value_pass(e - 1)
        softmax_pass(e)
    value_pass(heads[-1])


def _attention(qkv, za, slopes, lq1, lk1, lq2, lk2, subln_g, bsz, seq):
    blk = ATT_BLOCK
    n_blk = seq // blk
    hps = ATT_HEADS_PER_STEP
    ng = ATT_HEADS // hps
    w = hps * ATT_V_DIM
    vec = pl.BlockSpec((1, ATT_HEAD_DIM), lambda b, g: (0, 0))
    group = lambda first: pl.BlockSpec((seq, w), lambda b, g: (b, first + g))
    score_shapes = [((c + 1) * blk, 2 * blk) for c in range(n_blk)]
    return pl.pallas_call(
        _attn_kernel,
        grid=(bsz, ng),
        in_specs=[
            pl.BlockSpec(memory_space=pltpu.SMEM),
            pl.BlockSpec(memory_space=pltpu.SMEM),
            vec, vec, vec, vec,
            pl.BlockSpec((1, ATT_V_DIM), lambda b, g: (0, 0)),
            group(0), group(ng), group(2 * ng),
            group(0),
        ],
        out_specs=group(0),
        out_shape=jax.ShapeDtypeStruct((bsz * seq, ATT_HEADS * ATT_V_DIM), BF16),
        scratch_shapes=[
            pltpu.VMEM((hps, seq, 2 * ATT_V_DIM), BF16),
            pltpu.VMEM((hps, ATT_V_DIM + ATT_SUM_ROWS, seq), BF16),
        ] + [pltpu.VMEM(sh, F32) for sh in score_shapes]
          + [pltpu.VMEM(sh, BF16) for _ in range(hps) for sh in score_shapes],
        compiler_params=pltpu.CompilerParams(
            dimension_semantics=("arbitrary", "arbitrary"), vmem_limit_bytes=VMEM_LIMIT),
        name="diff_attention",
    )(slopes, jnp.zeros((1,), jnp.int32), lq1, lk1, lq2, lk2, subln_g, qkv, qkv, qkv, za)


def _s5_disc_kernel(lre_ref, lim_ref, ldt_ref, bre_ref, bim_ref, are_ref, aim_ref, bbre_ref, bbim_ref):
    dt = jnp.exp(ldt_ref[...])
    lre = jnp.minimum(lre_ref[...], -1e-4)
    lim = lim_ref[...]
    mag = jnp.exp(lre * dt)
    lbar_re = mag * jnp.cos(lim * dt)
    lbar_im = mag * jnp.sin(lim * dt)
    num_re = lbar_re - 1.0
    den = lre * lre + lim * lim
    coef_re = (num_re * lre + lbar_im * lim) / den
    coef_im = (lbar_im * lre - num_re * lim) / den
    bre = bre_ref[...]
    bim = bim_ref[...]
    are_ref[...] = lbar_re
    aim_ref[...] = lbar_im
    bbre_ref[...] = coef_re * bre - coef_im * bim
    bbim_ref[...] = coef_re * bim + coef_im * bre


def _s5_discretise(lam_re, lam_im, log_dt, b_re, b_im):
    g, p = lam_re.shape
    hg = b_re.shape[-1]
    n = g * p
    row = lambda a: a.reshape(1, n)
    tr = lambda a: a.reshape(n, hg).T
    ldt = jnp.broadcast_to(log_dt[:, None], (g, p))
    return pl.pallas_call(
        _s5_disc_kernel,
        out_shape=[
            jax.ShapeDtypeStruct((1, n), F32), jax.ShapeDtypeStruct((1, n), F32),
            jax.ShapeDtypeStruct((hg, n), F32), jax.ShapeDtypeStruct((hg, n), F32),
        ],
        name="s5_discretise",
    )(row(lam_re), row(lam_im), row(ldt), tr(b_re), tr(b_im))


def _s5_kernel(u_ref, zs_ref, perm_ref, are_ref, aim_ref, wb_ref, wc_ref, d_ref, wglu_ref, bglu_ref, o_ref,
               x_scr, st_scr, *, bsz, steps, n_chunks):
    @pl.when(pl.program_id(0) == 0)
    def _():
        st_scr[...] = jnp.zeros_like(st_scr)

    cs = CHUNK_STATE
    perm = perm_ref[...]
    perm_steps = perm.shape[0] // bsz
    pm = bsz * perm_steps
    u = jnp.concatenate(
        [jnp.dot(perm, u_ref[:, t0:t0 + perm_steps, :].reshape(pm, u_ref.shape[2]),
                 preferred_element_type=F32) for t0 in range(0, steps, perm_steps)], axis=0).astype(BF16)

    def project_in(c):
        x_scr[:, 2 * cs * c:2 * cs * (c + 1)] = jnp.dot(
            u[:, CHUNK_IN * c:CHUNK_IN * (c + 1)], wb_ref[c], preferred_element_type=F32)

    def scan(c):
        re0 = 2 * cs * c
        im0 = re0 + cs
        a_re = jnp.broadcast_to(are_ref[:, cs * c:cs * (c + 1)], (bsz, cs))
        a_im = jnp.broadcast_to(aim_ref[:, cs * c:cs * (c + 1)], (bsz, cs))

        def step(t, carry):
            xr, xi = carry
            r0 = pl.multiple_of(t * bsz, bsz)
            nr = a_re * xr - a_im * xi + x_scr[pl.ds(r0, bsz), re0:re0 + cs]
            ni = a_re * xi + a_im * xr + x_scr[pl.ds(r0, bsz), im0:im0 + cs]
            x_scr[pl.ds(r0, bsz), re0:re0 + cs] = nr
            x_scr[pl.ds(r0, bsz), im0:im0 + cs] = ni
            return nr, ni

        xr, xi = lax.fori_loop(0, steps, step,
                               (st_scr[:, re0:re0 + cs], st_scr[:, im0:im0 + cs]), unroll=True)
        st_scr[:, re0:re0 + cs] = xr
        st_scr[:, im0:im0 + cs] = xi

    def project_out(c):
        return jnp.dot(x_scr[:, 2 * cs * c:2 * cs * (c + 1)].astype(BF16), wc_ref[c],
                       preferred_element_type=F32)

    ys = [None] * n_chunks
    project_in(0)
    for c in range(n_chunks):
        if c + 1 < n_chunks:
            project_in(c + 1)
        if c >= 1:
            ys[c - 1] = project_out(c - 1)
        scan(c)
    ys[n_chunks - 1] = project_out(n_chunks - 1)
    y = jnp.concatenate(ys, axis=1) + d_ref[...] * u.astype(F32)
    y = jax.nn.gelu(y)
    z = jnp.dot(y.astype(BF16), wglu_ref[...], preferred_element_type=F32) + bglu_ref[...]
    out = (y * _sigmoid(z)).astype(BF16)
    for i, t0 in enumerate(range(0, steps, perm_steps)):
        part = lax.dot_general(perm, out[i * pm:(i + 1) * pm, :], (((0,), (0,)), ((), ())),
                               preferred_element_type=F32)
        o_ref[:, t0:t0 + perm_steps, :] = (part.astype(BF16).reshape(bsz, perm_steps, part.shape[1])
                                           * zs_ref[:, t0:t0 + perm_steps, :])


def _s5(u_bt, zs_bt, lbar_re, lbar_im, wb, wc, d_row, w_glu, b_glu, bsz, seq):
    ssm_w = u_bt.shape[2]
    n_chunks = wb.shape[0]
    steps = S5_STEPS
    m = steps * bsz
    n_state = n_chunks * CHUNK_STATE
    kern = functools.partial(_s5_kernel, bsz=bsz, steps=steps, n_chunks=n_chunks)
    const2 = lambda i: (0, 0)
    const3 = lambda i: (0, 0, 0)
    rows = pl.BlockSpec((bsz, steps, ssm_w), lambda i: (0, i, 0))
    pm = bsz * S5_PERM_STEPS
    r = jnp.arange(pm)
    perm = (r[None, :] == ((r % bsz) * S5_PERM_STEPS + r // bsz)[:, None]).astype(BF16)
    return pl.pallas_call(
        kern,
        grid=(seq // steps,),
        in_specs=[
            rows,
            rows,
            pl.BlockSpec((pm, pm), const2),
            pl.BlockSpec((1, n_state), const2),
            pl.BlockSpec((1, n_state), const2),
            pl.BlockSpec(wb.shape, const3),
            pl.BlockSpec(wc.shape, const3),
            pl.BlockSpec((1, ssm_w), const2),
            pl.BlockSpec(w_glu.shape, const2),
            pl.BlockSpec((1, ssm_w), const2),
        ],
        out_specs=rows,
        out_shape=jax.ShapeDtypeStruct((bsz, seq, ssm_w), BF16),
        scratch_shapes=[
            pltpu.VMEM((m, 2 * n_state), F32),
            pltpu.VMEM((bsz, 2 * n_state), F32),
        ],
        compiler_params=pltpu.CompilerParams(
            dimension_semantics=("arbitrary",), vmem_limit_bytes=VMEM_LIMIT),
        name="s5_scan",
    )(u_bt, zs_bt, perm, lbar_re, lbar_im, wb, wc, d_row, w_glu, b_glu)


def _s5_block_diag_weights(bbar_re, bbar_im, c_re, c_im):
    hg, n = bbar_re.shape
    g = n // SSM_STATE
    nc = g // GROUPS_PER_CHUNK
    eye = jnp.eye(GROUPS_PER_CHUNK, dtype=F32)

    def wb_part(bb):
        t = bb.reshape(hg, nc, GROUPS_PER_CHUNK, SSM_STATE)
        w = jnp.einsum('ab,hcbp->cahbp', eye, t)
        return w.reshape(nc, GROUPS_PER_CHUNK * hg, GROUPS_PER_CHUNK * SSM_STATE)

    def wc_part(cc):
        t = cc.reshape(nc, GROUPS_PER_CHUNK, hg, SSM_STATE)
        w = jnp.einsum('ab,cbhp->capbh', eye, t)
        return w.reshape(nc, GROUPS_PER_CHUNK * SSM_STATE, GROUPS_PER_CHUNK * hg)

    wb = jnp.concatenate([wb_part(bbar_re), wb_part(bbar_im)], axis=2).astype(BF16)
    wc = jnp.concatenate([wc_part(c_re), -wc_part(c_im)], axis=1).astype(BF16)
    return wb, wc


def _out_kernel(att_ref, ssm_ref, gate_ref, x_ref, woa_ref, wos_ref, wout_ref, fg_ref, o_ref, *, d_model):
    ya = jnp.dot(att_ref[...], woa_ref[...], preferred_element_type=F32)
    ys = jnp.dot(ssm_ref[...], wos_ref[...], preferred_element_type=F32)
    ga = gate_ref[:, 0:d_model].astype(F32)
    gs = gate_ref[:, d_model:2 * d_model].astype(F32)
    merged = ga * ya + gs * ys
    o = x_ref[...] + jnp.dot(merged.astype(BF16), wout_ref[...], preferred_element_type=F32)
    ms = jnp.mean(o * o, axis=-1, keepdims=True)
    o_ref[...] = o * lax.rsqrt(ms + EPS) * fg_ref[...]


def _out_proj(att, ssm, gate, x2, w_o_att, w_o_ssm, w_out, final_g, bsz, seq):
    d_model = x2.shape[1]
    att_w = att.shape[1]
    ssm_w = w_o_ssm.shape[0]
    tm = ROW_TILE
    nl = seq // tm
    row = lambda b, l: (b * nl + l, 0)
    const = lambda b, l: (0, 0)
    kern = functools.partial(_out_kernel, d_model=d_model)
    return pl.pallas_call(
        kern,
        grid=(bsz, nl),
        in_specs=[
            pl.BlockSpec((tm, att_w), row),
            pl.BlockSpec((tm, ssm_w), row),
            pl.BlockSpec((tm, 2 * d_model), row),
            pl.BlockSpec((tm, d_model), row),
            pl.BlockSpec(w_o_att.shape, const),
            pl.BlockSpec(w_o_ssm.shape, const),
            pl.BlockSpec(w_out.shape, const),
            pl.BlockSpec((1, d_model), const),
        ],
        out_specs=pl.BlockSpec((tm, d_model), row),
        out_shape=jax.ShapeDtypeStruct((bsz * seq, d_model), F32),
        compiler_params=pltpu.CompilerParams(
            dimension_semantics=("arbitrary", "arbitrary"), vmem_limit_bytes=VMEM_LIMIT),
        name="out_proj",
    )(att, ssm, gate, x2, w_o_att, w_o_ssm, w_out, final_g)


def kernel(x, norm_g, w_in, lambda_q1, lambda_k1, lambda_q2, lambda_k2, subln_g, w_o_att,
           ssm_lambda_re, ssm_lambda_im, ssm_log_dt, ssm_b_re, ssm_b_im, ssm_c_re, ssm_c_im,
           ssm_d, w_glu, b_glu, w_o_ssm, w_out, final_g):
    bsz, seq, d_model = x.shape
    depth = w_in.shape[0]
    assert depth == 1, "single-layer block"
    att_w = w_o_att.shape[1]
    ssm_w = w_o_ssm.shape[1]
    assert att_w == ATT_HEADS * ATT_V_DIM
    assert w_in.shape[2] == 4 * att_w + 2 * ssm_w + 2 * d_model
    assert seq % ROW_TILE == 0 and seq % ATT_BLOCK == 0 and seq % S5_STEPS == 0
    assert S5_STEPS % S5_PERM_STEPS == 0
    assert bsz % 16 == 0, "batch rows fill whole bf16 sublane tiles in the S5 kernel"
    assert (ssm_w // SSM_GROUP) % GROUPS_PER_CHUNK == 0

    x2 = x.reshape(bsz * seq, d_model)
    qkv, za, u, zs, gate = _in_proj(x2, norm_g[0][None], w_in[0].astype(BF16), bsz, seq, att_w, ssm_w)

    slopes = jnp.asarray([2.0 ** (-8.0 * (h + 1) / ATT_HEADS) for h in range(ATT_HEADS)], F32)
    att = _attention(qkv, za, slopes, lambda_q1, lambda_k1, lambda_q2, lambda_k2, subln_g, bsz, seq)

    lbar_re, lbar_im, bbar_re, bbar_im = _s5_discretise(
        ssm_lambda_re[0], ssm_lambda_im[0], ssm_log_dt[0], ssm_b_re[0], ssm_b_im[0])
    wb, wc = _s5_block_diag_weights(bbar_re, bbar_im, ssm_c_re[0], ssm_c_im[0])
    ssm = _s5(u.reshape(bsz, seq, ssm_w), zs.reshape(bsz, seq, ssm_w), lbar_re, lbar_im, wb, wc,
              ssm_d[0].reshape(1, ssm_w), w_glu[0].astype(BF16), b_glu[0][None], bsz, seq)

    out = _out_proj(att, ssm.reshape(bsz * seq, ssm_w), gate, x2,
                    w_o_att[0].astype(BF16), w_o_ssm[0].astype(BF16), w_out[0].astype(BF16),
                    final_g[None], bsz, seq)
    return out.reshape(bsz, seq, d_model)
```

```python
import functools
import math

import jax
import jax.numpy as jnp
from jax import lax
from jax.experimental import pallas as pl
from jax.experimental.pallas import tpu as pltpu

EPS = 1e-5
ATT_HEADS = 8
ATT_HEAD_DIM = 64
ATT_V_DIM = 2 * ATT_HEAD_DIM
SSM_GROUP = 16
SSM_STATE = 64
LAMBDA_INIT = 0.8 - 0.6 * math.exp(-0.3 * 0)
LOG2E = math.log2(math.e)
Q_PRESCALE = ATT_HEAD_DIM ** -0.5 * LOG2E
BIAS_TERMS = 3

GROUPS_PER_CHUNK = 8
CHUNK_IN = GROUPS_PER_CHUNK * SSM_GROUP
CHUNK_STATE = GROUPS_PER_CHUNK * SSM_STATE

ROW_TILE = 1024
ATT_BLOCK = 128
ATT_SUM_ROWS = 16
ATT_HEADS_PER_STEP = 1
S5_STEPS = 64
S5_PERM_STEPS = 32
VMEM_LIMIT = 56 * 1024 * 1024

F32 = jnp.float32
BF16 = jnp.bfloat16


def _sigmoid(x):
    return 0.5 * jnp.tanh(0.5 * x) + 0.5


def _silu(x):
    return x * _sigmoid(x)


def _in_proj_kernel(x_ref, g_ref, w_ref, qkv_ref, za_ref, u_ref, zs_ref, gate_ref, *, att_w, ssm_w, d_model):
    x = x_ref[...]
    ms = jnp.mean(x * x, axis=-1, keepdims=True)
    h = (x * lax.rsqrt(ms + EPS) * g_ref[...]).astype(BF16)

    def proj(dst_ref, dst0, src0, width, fn=None, step=512):
        for o in range(0, width, step):
            r = jnp.dot(h, w_ref[:, src0 + o:src0 + o + step], preferred_element_type=F32)
            if fn is not None:
                r = fn(r)
            dst_ref[:, dst0 + o:dst0 + o + step] = r.astype(dst_ref.dtype)

    q0, z0 = 0, 3 * att_w
    u0 = z0 + att_w
    r0 = u0 + ssm_w
    proj(za_ref, 0, z0, att_w, fn=_silu)
    proj(zs_ref, 0, r0, ssm_w, fn=_silu)
    proj(gate_ref, 0, r0 + ssm_w, 2 * d_model, fn=_sigmoid)
    proj(qkv_ref, 0, q0, att_w, fn=lambda r: r * Q_PRESCALE)
    proj(u_ref, 0, u0, ssm_w)
    proj(qkv_ref, att_w, q0 + att_w, 2 * att_w)


def _in_proj(x2, norm_g, w_in, bsz, seq, att_w, ssm_w):
    d_model = x2.shape[1]
    tm = ROW_TILE
    nl = seq // tm
    kern = functools.partial(_in_proj_kernel, att_w=att_w, ssm_w=ssm_w, d_model=d_model)
    row = lambda b, l: (b * nl + l, 0)
    return pl.pallas_call(
        kern,
        grid=(bsz, nl),
        in_specs=[
            pl.BlockSpec((tm, d_model), row),
            pl.BlockSpec((1, d_model), lambda b, l: (0, 0)),
            pl.BlockSpec(w_in.shape, lambda b, l: (0, 0), pipeline_mode=pl.Buffered(1)),
        ],
        out_specs=[
            pl.BlockSpec((tm, 3 * att_w), row),
            pl.BlockSpec((tm, att_w), row),
            pl.BlockSpec((tm, ssm_w), row),
            pl.BlockSpec((tm, ssm_w), row),
            pl.BlockSpec((tm, 2 * d_model), row),
        ],
        out_shape=[
            jax.ShapeDtypeStruct((bsz * seq, 3 * att_w), BF16),
            jax.ShapeDtypeStruct((bsz * seq, att_w), BF16),
            jax.ShapeDtypeStruct((bsz * seq, ssm_w), BF16),
            jax.ShapeDtypeStruct((bsz * seq, ssm_w), BF16),
            jax.ShapeDtypeStruct((bsz * seq, 2 * d_model), BF16),
        ],
        compiler_params=pltpu.CompilerParams(
            dimension_semantics=("arbitrary", "arbitrary"), vmem_limit_bytes=VMEM_LIMIT),
        name="in_proj",
    )(x2, norm_g, w_in)


def _split_bf16(v, n):
    parts = []
    for _ in range(n):
        p = v.astype(BF16)
        parts.append(p)
        v = v - p.astype(F32)
    return parts


def _attn_kernel(slopes_ref, zero_ref, lq1_ref, lk1_ref, lq2_ref, lk2_ref, sg_ref, q_ref, k_ref, v_ref,
                 za_ref, o_ref, kx_ref, vt_ref, *sp_refs):
    blk = ATT_BLOCK
    hd = ATT_HEAD_DIM
    vd = ATT_V_DIM
    n_blk = k_ref.shape[0] // blk
    heads = range(ATT_HEADS_PER_STEP)
    s_refs = sp_refs[:n_blk]
    p_refs = [sp_refs[(1 + e) * n_blk:(2 + e) * n_blk] for e in heads]

    lane = lax.broadcasted_iota(jnp.int32, (blk, vd), 1)
    pos = lax.broadcasted_iota(jnp.int32, (blk, vd), 0)
    lane_row = lax.broadcasted_iota(jnp.int32, (1, vd), 1)
    sub = lax.broadcasted_iota(jnp.int32, (ATT_SUM_ROWS, vt_ref.shape[2]), 0)
    for e in heads:
        slope2 = slopes_ref[pl.program_id(1) * ATT_HEADS_PER_STEP + e] * LOG2E
        local = jnp.zeros((blk, vd), F32)
        for t, p in enumerate(_split_bf16(slope2 * pos.astype(F32), BIAS_TERMS)):
            local = jnp.where(lane == t, p.astype(F32), local)
        for j in range(n_blk):
            rows = pl.ds(j * blk, blk)
            offset = jnp.zeros((1, vd), F32)
            for t, p in enumerate(_split_bf16(slope2 * jnp.full((1, vd), j * blk, F32), BIAS_TERMS)):
                offset = jnp.where(lane_row == BIAS_TERMS + t, p.astype(F32), offset)
            kx_ref[e, rows, 0:vd] = k_ref[rows, e * vd:(e + 1) * vd]
            kx_ref[e, rows, vd:2 * vd] = (local + offset).astype(BF16)
            vt_ref[e, 0:vd, j * blk:(j + 1) * blk] = (
                v_ref[rows, e * vd:(e + 1) * vd].astype(F32).T.astype(BF16))
        vt_ref[e, vd:vd + ATT_SUM_ROWS, :] = jnp.where(sub == 0, 1.0, 0.0).astype(BF16)

    e1 = jnp.exp(jnp.sum(lq1_ref[...] * lk1_ref[...], axis=-1, keepdims=True))
    e2 = jnp.exp(jnp.sum(lq2_ref[...] * lk2_ref[...], axis=-1, keepdims=True))
    lam = e1 - e2 + LAMBDA_INIT
    gain = sg_ref[...] * (1.0 - LAMBDA_INIT)
    nt = (((1,), (1,)), ((), ()))

    key = lax.broadcasted_iota(jnp.int32, (blk, 2 * blk), 0)
    qry = lax.broadcasted_iota(jnp.int32, (blk, 2 * blk), 1)
    causal = key <= jnp.where(qry >= blk, qry - blk, qry)

    dyn0 = zero_ref[0]
    order = list(reversed(range(n_blk)))
    col_max = {}

    def score_pass(e):
        for c in order:
            nk = (c + 1) * blk
            q = q_ref[c * blk:(c + 1) * blk, e * vd:(e + 1) * vd].astype(F32)
            zero = jnp.zeros_like(q)
            ones = jnp.where(lane < 2 * BIAS_TERMS, jnp.ones_like(q), zero)
            qx = jnp.concatenate([
                jnp.concatenate([jnp.where(lane < hd, q, zero), ones], axis=1),
                jnp.concatenate([jnp.where(lane >= hd, q, zero), ones], axis=1)], axis=0).astype(BF16)
            s = lax.dot_general(kx_ref[e, 0:nk, :], qx, nt, preferred_element_type=F32)
            m = None
            for j in range(c + 1):
                sj = s[j * blk:(j + 1) * blk, :]
                if j == c:
                    sj = jnp.where(causal, sj, -jnp.inf)
                s_refs[c][j * blk:(j + 1) * blk, :] = sj
                mj = jnp.max(sj, axis=0, keepdims=True)
                m = mj if m is None else jnp.maximum(m, mj)
            col_max[e, c] = m

    def softmax_pass(e):
        for c in order:
            for j in range(c + 1):
                sj = s_refs[c][pl.ds(pl.multiple_of(dyn0 + j * blk, blk), blk), :]
                p_refs[e][c][j * blk:(j + 1) * blk, :] = jnp.exp2((sj - col_max[e, c]).astype(BF16))

    def value_pass(e):
        for c in order:
            nk = (c + 1) * blk
            p = p_refs[e][c][pl.ds(pl.multiple_of(dyn0, blk), nk), :]
            acc = jnp.dot(vt_ref[e, :, 0:nk], p, preferred_element_type=F32)
            r = 1.0 / acc[vd:vd + 1, :]
            ot = acc[0:vd, 0:blk] * r[:, 0:blk] - acc[0:vd, blk:2 * blk] * (lam * r[:, blk:2 * blk])
            ms = jnp.mean(ot * ot, axis=0, keepdims=True)
            att = (ot * lax.rsqrt(ms + EPS)).T * gain
            o_ref[c * blk:(c + 1) * blk, e * vd:(e + 1) * vd] = (
                att.astype(BF16) * za_ref[c * blk:(c + 1) * blk, e * vd:(e + 1) * vd])

    score_pass(0)
    softmax_pass(0)
    for e in heads[1:]:
        score_pass(e)
        value_pass(e - 1)
        softmax_pass(e)
    value_pass(heads[-1])


def _attention(qkv, za, slopes, lq1, lk1, lq2, lk2, subln_g, bsz, seq):
    blk = ATT_BLOCK
    n_blk = seq // blk
    hps = ATT_HEADS_PER_STEP
    ng = ATT_HEADS // hps
    w = hps * ATT_V_DIM
    vec = pl.BlockSpec((1, ATT_HEAD_DIM), lambda b, g: (0, 0))
    group = lambda first: pl.BlockSpec((seq, w), lambda b, g: (b, first + g))
    score_shapes = [((c + 1) * blk, 2 * blk) for c in range(n_blk)]
    return pl.pallas_call(
        _attn_kernel,
        grid=(bsz, ng),
        in_specs=[
            pl.BlockSpec(memory_space=pltpu.SMEM),
            pl.BlockSpec(memory_space=pltpu.SMEM),
            vec, vec, vec, vec,
            pl.BlockSpec((1, ATT_V_DIM), lambda b, g: (0, 0)),
            group(0), group(ng), group(2 * ng),
            group(0),
        ],
        out_specs=group(0),
        out_shape=jax.ShapeDtypeStruct((bsz * seq, ATT_HEADS * ATT_V_DIM), BF16),
        scratch_shapes=[
            pltpu.VMEM((hps, seq, 2 * ATT_V_DIM), BF16),
            pltpu.VMEM((hps, ATT_V_DIM + ATT_SUM_ROWS, seq), BF16),
        ] + [pltpu.VMEM(sh, F32) for sh in score_shapes]
          + [pltpu.VMEM(sh, BF16) for _ in range(hps) for sh in score_shapes],
        compiler_params=pltpu.CompilerParams(
            dimension_semantics=("arbitrary", "arbitrary"), vmem_limit_bytes=VMEM_LIMIT),
        name="diff_attention",
    )(slopes, jnp.zeros((1,), jnp.int32), lq1, lk1, lq2, lk2, subln_g, qkv, qkv, qkv, za)


def _s5_disc_kernel(lre_ref, lim_ref, ldt_ref, bre_ref, bim_ref, are_ref, aim_ref, bbre_ref, bbim_ref):
    dt = jnp.exp(ldt_ref[...])
    lre = jnp.minimum(lre_ref[...], -1e-4)
    lim = lim_ref[...]
    mag = jnp.exp(lre * dt)
    lbar_re = mag * jnp.cos(lim * dt)
    lbar_im = mag * jnp.sin(lim * dt)
    num_re = lbar_re - 1.0
    den = lre * lre + lim * lim
    coef_re = (num_re * lre + lbar_im * lim) / den
    coef_im = (lbar_im * lre - num_re * lim) / den
    bre = bre_ref[...]
    bim = bim_ref[...]
    are_ref[...] = lbar_re
    aim_ref[...] = lbar_im
    bbre_ref[...] = coef_re * bre - coef_im * bim
    bbim_ref[...] = coef_re * bim + coef_im * bre


def _s5_discretise(lam_re, lam_im, log_dt, b_re, b_im):
    g, p = lam_re.shape
    hg = b_re.shape[-1]
    n = g * p
    row = lambda a: a.reshape(1, n)
    tr = lambda a: a.reshape(n, hg).T
    ldt = jnp.broadcast_to(log_dt[:, None], (g, p))
    return pl.pallas_call(
        _s5_disc_kernel,
        out_shape=[
            jax.ShapeDtypeStruct((1, n), F32), jax.ShapeDtypeStruct((1, n), F32),
            jax.ShapeDtypeStruct((hg, n), F32), jax.ShapeDtypeStruct((hg, n), F32),
        ],
        name="s5_discretise",
    )(row(lam_re), row(lam_im), row(ldt), tr(b_re), tr(b_im))


def _s5_kernel(u_ref, zs_ref, perm_ref, are_ref, aim_ref, wb_ref, wc_ref, d_ref, wglu_ref, bglu_ref, o_ref,
               x_scr, st_scr, *, bsz, steps, n_chunks):
    @pl.when(pl.program_id(0) == 0)
    def _():
        st_scr[...] = jnp.zeros_like(st_scr)

    cs = CHUNK_STATE
    perm = perm_ref[...]
    perm_steps = perm.shape[0] // bsz
    pm = bsz * perm_steps
    u = jnp.concatenate(
        [jnp.dot(perm, u_ref[:, t0:t0 + perm_steps, :].reshape(pm, u_ref.shape[2]),
                 preferred_element_type=F32) for t0 in range(0, steps, perm_steps)], axis=0).astype(BF16)

    def project_in(c):
        x_scr[:, 2 * cs * c:2 * cs * (c + 1)] = jnp.dot(
            u[:, CHUNK_IN * c:CHUNK_IN * (c + 1)], wb_ref[c], preferred_element_type=F32)

    def scan(c):
        re0 = 2 * cs * c
        im0 = re0 + cs
        a_re = jnp.broadcast_to(are_ref[:, cs * c:cs * (c + 1)], (bsz, cs))
        a_im = jnp.broadcast_to(aim_ref[:, cs * c:cs * (c + 1)], (bsz, cs))

        def step(t, carry):
            xr, xi = carry
            r0 = pl.multiple_of(t * bsz, bsz)
            nr = a_re * xr - a_im * xi + x_scr[pl.ds(r0, bsz), re0:re0 + cs]
            ni = a_re * xi + a_im * xr + x_scr[pl.ds(r0, bsz), im0:im0 + cs]
            x_scr[pl.ds(r0, bsz), re0:re0 + cs] = nr
            x_scr[pl.ds(r0, bsz), im0:im0 + cs] = ni
            return nr, ni

        xr, xi = lax.fori_loop(0, steps, step,
                               (st_scr[:, re0:re0 + cs], st_scr[:, im0:im0 + cs]), unroll=True)
        st_scr[:, re0:re0 + cs] = xr
        st_scr[:, im0:im0 + cs] = xi

    def project_out(c):
        return jnp.dot(x_scr[:, 2 * cs * c:2 * cs * (c + 1)].astype(BF16), wc_ref[c],
                       preferred_element_type=F32)

    ys = [None] * n_chunks
    project_in(0)
    for c in range(n_chunks):
        if c + 1 < n_chunks:
            project_in(c + 1)
        if c >= 1:
            ys[c - 1] = project_out(c - 1)
        scan(c)
    ys[n_chunks - 1] = project_out(n_chunks - 1)
    y = jnp.concatenate(ys, axis=1) + d_ref[...] * u.astype(F32)
    y = jax.nn.gelu(y)
    z = jnp.dot(y.astype(BF16), wglu_ref[...], preferred_element_type=F32) + bglu_ref[...]
    out = (y * _sigmoid(z)).astype(BF16)
    for i, t0 in enumerate(range(0, steps, perm_steps)):
        part = lax.dot_general(perm, out[i * pm:(i + 1) * pm, :], (((0,), (0,)), ((), ())),
                               preferred_element_type=F32)
        o_ref[:, t0:t0 + perm_steps, :] = (part.astype(BF16).reshape(bsz, perm_steps, part.shape[1])
                                           * zs_ref[:, t0:t0 + perm_steps, :])


def _s5(u_bt, zs_bt, lbar_re, lbar_im, wb, wc, d_row, w_glu, b_glu, bsz, seq):
    ssm_w = u_bt.shape[2]
    n_chunks = wb.shape[0]
    steps = S5_STEPS
    m = steps * bsz
    n_state = n_chunks * CHUNK_STATE
    kern = functools.partial(_s5_kernel, bsz=bsz, steps=steps, n_chunks=n_chunks)
    const2 = lambda i: (0, 0)
    const3 = lambda i: (0, 0, 0)
    rows = pl.BlockSpec((bsz, steps, ssm_w), lambda i: (0, i, 0))
    pm = bsz * S5_PERM_STEPS
    r = jnp.arange(pm)
    perm = (r[None, :] == ((r % bsz) * S5_PERM_STEPS + r // bsz)[:, None]).astype(BF16)
    return pl.pallas_call(
        kern,
        grid=(seq // steps,),
        in_specs=[
            rows,
            rows,
            pl.BlockSpec((pm, pm), const2),
            pl.BlockSpec((1, n_state), const2),
            pl.BlockSpec((1, n_state), const2),
            pl.BlockSpec(wb.shape, const3),
            pl.BlockSpec(wc.shape, const3),
            pl.BlockSpec((1, ssm_w), const2),
            pl.BlockSpec(w_glu.shape, const2),
            pl.BlockSpec((1, ssm_w), const2),
        ],
        out_specs=rows,
        out_shape=jax.ShapeDtypeStruct((bsz, seq, ssm_w), BF16),
        scratch_shapes=[
            pltpu.VMEM((m, 2 * n_state), F32),
            pltpu.VMEM((bsz, 2 * n_state), F32),
        ],
        compiler_params=pltpu.CompilerParams(
            dimension_semantics=("arbitrary",), vmem_limit_bytes=VMEM_LIMIT),
        name="s5_scan",
    )(u_bt, zs_bt, perm, lbar_re, lbar_im, wb, wc, d_row, w_glu, b_glu)


def _s5_block_diag_weights(bbar_re, bbar_im, c_re, c_im):
    hg, n = bbar_re.shape
    g = n // SSM_STATE
    nc = g // GROUPS_PER_CHUNK
    eye = jnp.eye(GROUPS_PER_CHUNK, dtype=F32)

    def wb_part(bb):
        t = bb.reshape(hg, nc, GROUPS_PER_CHUNK, SSM_STATE)
        w = jnp.einsum('ab,hcbp->cahbp', eye, t)
        return w.reshape(nc, GROUPS_PER_CHUNK * hg, GROUPS_PER_CHUNK * SSM_STATE)

    def wc_part(cc):
        t = cc.reshape(nc, GROUPS_PER_CHUNK, hg, SSM_STATE)
        w = jnp.einsum('ab,cbhp->capbh', eye, t)
        return w.reshape(nc, GROUPS_PER_CHUNK * SSM_STATE, GROUPS_PER_CHUNK * hg)

    wb = jnp.concatenate([wb_part(bbar_re), wb_part(bbar_im)], axis=2).astype(BF16)
    wc = jnp.concatenate([wc_part(c_re), -wc_part(c_im)], axis=1).astype(BF16)
    return wb, wc


def _out_kernel(att_ref, ssm_ref, gate_ref, x_ref, woa_ref, wos_ref, wout_ref, fg_ref, o_ref, *, d_model):
    ya = jnp.dot(att_ref[...], woa_ref[...], preferred_element_type=F32)
    ys = jnp.dot(ssm_ref[...], wos_ref[...], preferred_element_type=F32)
    ga = gate_ref[:, 0:d_model].astype(F32)
    gs = gate_ref[:, d_model:2 * d_model].astype(F32)
    merged = ga * ya + gs * ys
    o = x_ref[...] + jnp.dot(merged.astype(BF16), wout_ref[...], preferred_element_type=F32)
    ms = jnp.mean(o * o, axis=-1, keepdims=True)
    o_ref[...] = o * lax.rsqrt(ms + EPS) * fg_ref[...]


def _out_proj(att, ssm, gate, x2, w_o_att, w_o_ssm, w_out, final_g, bsz, seq):
    d_model = x2.shape[1]
    att_w = att.shape[1]
    ssm_w = w_o_ssm.shape[0]
    tm = ROW_TILE
    nl = seq // tm
    row = lambda b, l: (b * nl + l, 0)
    const = lambda b, l: (0, 0)
    kern = functools.partial(_out_kernel, d_model=d_model)
    return pl.pallas_call(
        kern,
        grid=(bsz, nl),
        in_specs=[
            pl.BlockSpec((tm, att_w), row),
            pl.BlockSpec((tm, ssm_w), row),
            pl.BlockSpec((tm, 2 * d_model), row),
            pl.BlockSpec((tm, d_model), row),
            pl.BlockSpec(w_o_att.shape, const),
            pl.BlockSpec(w_o_ssm.shape, const),
            pl.BlockSpec(w_out.shape, const),
            pl.BlockSpec((1, d_model), const),
        ],
        out_specs=pl.BlockSpec((tm, d_model), row),
        out_shape=jax.ShapeDtypeStruct((bsz * seq, d_model), F32),
        compiler_params=pltpu.CompilerParams(
            dimension_semantics=("arbitrary", "arbitrary"), vmem_limit_bytes=VMEM_LIMIT),
        name="out_proj",
    )(att, ssm, gate, x2, w_o_att, w_o_ssm, w_out, final_g)


def kernel(x, norm_g, w_in, lambda_q1, lambda_k1, lambda_q2, lambda_k2, subln_g, w_o_att,
           ssm_lambda_re, ssm_lambda_im, ssm_log_dt, ssm_b_re, ssm_b_im, ssm_c_re, ssm_c_im,
           ssm_d, w_glu, b_glu, w_o_ssm, w_out, final_g):
    bsz, seq, d_model = x.shape
    depth = w_in.shape[0]
    assert depth == 1, "single-layer block"
    att_w = w_o_att.shape[1]
    ssm_w = w_o_ssm.shape[1]
    assert att_w == ATT_HEADS * ATT_V_DIM
    assert w_in.shape[2] == 4 * att_w + 2 * ssm_w + 2 * d_model
    assert seq % ROW_TILE == 0 and seq % ATT_BLOCK == 0 and seq % S5_STEPS == 0
    assert S5_STEPS % S5_PERM_STEPS == 0
    assert bsz % 16 == 0, "batch rows fill whole bf16 sublane tiles in the S5 kernel"
    assert (ssm_w // SSM_GROUP) % GROUPS_PER_CHUNK == 0

    x2 = x.reshape(bsz * seq, d_model)
    qkv, za, u, zs, gate = _in_proj(x2, norm_g[0][None], w_in[0].astype(BF16), bsz, seq, att_w, ssm_w)

    slopes = jnp.asarray([2.0 ** (-8.0 * (h + 1) / ATT_HEADS) for h in range(ATT_HEADS)], F32)
    att = _attention(qkv, za, slopes, lambda_q1, lambda_k1, lambda_q2, lambda_k2, subln_g, bsz, seq)

    lbar_re, lbar_im, bbar_re, bbar_im = _s5_discretise(
        ssm_lambda_re[0], ssm_lambda_im[0], ssm_log_dt[0], ssm_b_re[0], ssm_b_im[0])
    wb, wc = _s5_block_diag_weights(bbar_re, bbar_im, ssm_c_re[0], ssm_c_im[0])
    ssm = _s5(u.reshape(bsz, seq, ssm_w), zs.reshape(bsz, seq, ssm_w), lbar_re, lbar_im, wb, wc,
              ssm_d[0].reshape(1, ssm_w), w_glu[0].astype(BF16), b_glu[0][None], bsz, seq)

    out = _out_proj(att, ssm.reshape(bsz * seq, ssm_w), gate, x2,
                    w_o_att[0].astype(BF16), w_o_ssm[0].astype(BF16), w_out[0].astype(BF16),
                    final_g[None], bsz, seq)
    return out.reshape(bsz, seq, d_model)
```

```python
import functools
import math

import jax
import jax.numpy as jnp
from jax import lax
from jax.experimental import pallas as pl
from jax.experimental.pallas import tpu as pltpu

EPS = 1e-5
ATT_HEADS = 8
ATT_HEAD_DIM = 64
ATT_V_DIM = 2 * ATT_HEAD_DIM
SSM_GROUP = 16
SSM_STATE = 64
LAMBDA_INIT = 0.8 - 0.6 * math.exp(-0.3 * 0)
LOG2E = math.log2(math.e)
Q_PRESCALE = ATT_HEAD_DIM ** -0.5 * LOG2E
BIAS_TERMS = 3

GROUPS_PER_CHUNK = 8
CHUNK_IN = GROUPS_PER_CHUNK * SSM_GROUP
CHUNK_STATE = GROUPS_PER_CHUNK * SSM_STATE

ROW_TILE = 1024
ATT_BLOCK = 128
ATT_SUM_ROWS = 16
ATT_VALUE_LAG = 7
S5_STEPS = 64
S5_PERM_STEPS = 32
VMEM_LIMIT = 56 * 1024 * 1024

F32 = jnp.float32
BF16 = jnp.bfloat16


def _sigmoid(x):
    return 0.5 * jnp.tanh(0.5 * x) + 0.5


def _silu(x):
    return x * _sigmoid(x)


def _in_proj_kernel(x_ref, g_ref, w_ref, qkv_ref, za_ref, u_ref, zs_ref, gate_ref, *, att_w, ssm_w, d_model):
    x = x_ref[...]
    ms = jnp.mean(x * x, axis=-1, keepdims=True)
    h = (x * lax.rsqrt(ms + EPS) * g_ref[...]).astype(BF16)

    def proj(dst_ref, dst0, src0, width, fn=None, step=512):
        for o in range(0, width, step):
            r = jnp.dot(h, w_ref[:, src0 + o:src0 + o + step], preferred_element_type=F32)
            if fn is not None:
                r = fn(r)
            dst_ref[:, dst0 + o:dst0 + o + step] = r.astype(dst_ref.dtype)

    q0, z0 = 0, 3 * att_w
    u0 = z0 + att_w
    r0 = u0 + ssm_w
    proj(za_ref, 0, z0, att_w, fn=_silu)
    proj(zs_ref, 0, r0, ssm_w, fn=_silu)
    proj(gate_ref, 0, r0 + ssm_w, 2 * d_model, fn=_sigmoid)
    proj(qkv_ref, 0, q0, att_w, fn=lambda r: r * Q_PRESCALE)
    proj(u_ref, 0, u0, ssm_w)
    proj(qkv_ref, att_w, q0 + att_w, 2 * att_w)


def _in_proj(x2, norm_g, w_in, bsz, seq, att_w, ssm_w):
    d_model = x2.shape[1]
    tm = ROW_TILE
    nl = seq // tm
    kern = functools.partial(_in_proj_kernel, att_w=att_w, ssm_w=ssm_w, d_model=d_model)
    row = lambda b, l: (b * nl + l, 0)
    return pl.pallas_call(
        kern,
        grid=(bsz, nl),
        in_specs=[
            pl.BlockSpec((tm, d_model), row),
            pl.BlockSpec((1, d_model), lambda b, l: (0, 0)),
            pl.BlockSpec(w_in.shape, lambda b, l: (0, 0), pipeline_mode=pl.Buffered(1)),
        ],
        out_specs=[
            pl.BlockSpec((tm, 3 * att_w), row),
            pl.BlockSpec((tm, att_w), row),
            pl.BlockSpec((tm, ssm_w), row),
            pl.BlockSpec((tm, ssm_w), row),
            pl.BlockSpec((tm, 2 * d_model), row),
        ],
        out_shape=[
            jax.ShapeDtypeStruct((bsz * seq, 3 * att_w), BF16),
            jax.ShapeDtypeStruct((bsz * seq, att_w), BF16),
            jax.ShapeDtypeStruct((bsz * seq, ssm_w), BF16),
            jax.ShapeDtypeStruct((bsz * seq, ssm_w), BF16),
            jax.ShapeDtypeStruct((bsz * seq, 2 * d_model), BF16),
        ],
        compiler_params=pltpu.CompilerParams(
            dimension_semantics=("arbitrary", "arbitrary"), vmem_limit_bytes=VMEM_LIMIT),
        name="in_proj",
    )(x2, norm_g, w_in)


def _split_bf16(v, n):
    parts = []
    for _ in range(n):
        p = v.astype(BF16)
        parts.append(p)
        v = v - p.astype(F32)
    return parts


def _attn_kernel(slopes_ref, zero_ref, lq1_ref, lk1_ref, lq2_ref, lk2_ref, sg_ref, q_ref, k_ref, v_ref,
                 za_ref, o_ref, kx_ref, vt_ref, *sp_refs):
    blk = ATT_BLOCK
    hd = ATT_HEAD_DIM
    vd = ATT_V_DIM
    n_blk = k_ref.shape[0] // blk
    s_refs, p_refs = sp_refs[:n_blk], sp_refs[n_blk:]

    slope2 = slopes_ref[pl.program_id(1)] * LOG2E
    lane = lax.broadcasted_iota(jnp.int32, (blk, vd), 1)
    pos = lax.broadcasted_iota(jnp.int32, (blk, vd), 0)
    local = jnp.zeros((blk, vd), F32)
    for t, p in enumerate(_split_bf16(slope2 * pos.astype(F32), BIAS_TERMS)):
        local = jnp.where(lane == t, p.astype(F32), local)
    lane_row = lax.broadcasted_iota(jnp.int32, (1, vd), 1)
    for j in range(n_blk):
        rows = pl.ds(j * blk, blk)
        offset = jnp.zeros((1, vd), F32)
        for t, p in enumerate(_split_bf16(slope2 * jnp.full((1, vd), j * blk, F32), BIAS_TERMS)):
            offset = jnp.where(lane_row == BIAS_TERMS + t, p.astype(F32), offset)
        kx_ref[rows, 0:vd] = k_ref[rows, :]
        kx_ref[rows, vd:2 * vd] = (local + offset).astype(BF16)
        vt_ref[0:vd, j * blk:(j + 1) * blk] = v_ref[rows, :].astype(F32).T.astype(BF16)
    sub = lax.broadcasted_iota(jnp.int32, (ATT_SUM_ROWS, vt_ref.shape[1]), 0)
    vt_ref[vd:vd + ATT_SUM_ROWS, :] = jnp.where(sub == 0, 1.0, 0.0).astype(BF16)

    e1 = jnp.exp(jnp.sum(lq1_ref[...] * lk1_ref[...], axis=-1, keepdims=True))
    e2 = jnp.exp(jnp.sum(lq2_ref[...] * lk2_ref[...], axis=-1, keepdims=True))
    lam = e1 - e2 + LAMBDA_INIT
    gain = sg_ref[...] * (1.0 - LAMBDA_INIT)
    nt = (((1,), (1,)), ((), ()))

    key = lax.broadcasted_iota(jnp.int32, (blk, 2 * blk), 0)
    qry = lax.broadcasted_iota(jnp.int32, (blk, 2 * blk), 1)
    causal = key <= jnp.where(qry >= blk, qry - blk, qry)

    dyn0 = zero_ref[0]
    col_max = {}

    def score_stage(c):
        nk = (c + 1) * blk
        q = q_ref[c * blk:(c + 1) * blk, :].astype(F32)
        zero = jnp.zeros_like(q)
        ones = jnp.where(lane < 2 * BIAS_TERMS, jnp.ones_like(q), zero)
        qx = jnp.concatenate([
            jnp.concatenate([jnp.where(lane < hd, q, zero), ones], axis=1),
            jnp.concatenate([jnp.where(lane >= hd, q, zero), ones], axis=1)], axis=0).astype(BF16)
        s = lax.dot_general(kx_ref[0:nk, :], qx, nt, preferred_element_type=F32)
        m = None
        for j in range(c + 1):
            sj = s[j * blk:(j + 1) * blk, :]
            if j == c:
                sj = jnp.where(causal, sj, -jnp.inf)
            s_refs[c][j * blk:(j + 1) * blk, :] = sj
            mj = jnp.max(sj, axis=0, keepdims=True)
            m = mj if m is None else jnp.maximum(m, mj)
        col_max[c] = m

    def softmax_stage(c):
        for j in range(c + 1):
            sj = s_refs[c][pl.ds(pl.multiple_of(dyn0 + j * blk, blk), blk), :]
            p_refs[c][j * blk:(j + 1) * blk, :] = jnp.exp2(sj - col_max[c]).astype(BF16)

    def value_stage(c):
        nk = (c + 1) * blk
        p = p_refs[c][pl.ds(pl.multiple_of(dyn0, blk), nk), :]
        acc = jnp.dot(vt_ref[:, 0:nk], p, preferred_element_type=F32)
        r = 1.0 / acc[vd:vd + 1, :]
        ot = acc[0:vd, 0:blk] * r[:, 0:blk] - acc[0:vd, blk:2 * blk] * (lam * r[:, blk:2 * blk])
        ms = jnp.mean(ot * ot, axis=0, keepdims=True)
        att = (ot * lax.rsqrt(ms + EPS)).T * gain
        o_ref[c * blk:(c + 1) * blk, :] = att.astype(BF16) * za_ref[c * blk:(c + 1) * blk, :]

    order = list(reversed(range(n_blk)))
    for t in range(n_blk + ATT_VALUE_LAG):
        if t < n_blk:
            score_stage(order[t])
            softmax_stage(order[t])
        if t >= ATT_VALUE_LAG:
            value_stage(order[t - ATT_VALUE_LAG])


def _attention(qkv, za, slopes, lq1, lk1, lq2, lk2, subln_g, bsz, seq):
    blk = ATT_BLOCK
    n_blk = seq // blk
    nh = ATT_HEADS
    vec = pl.BlockSpec((1, ATT_HEAD_DIM), lambda b, h: (0, 0))
    head = lambda first: pl.BlockSpec((seq, ATT_V_DIM), lambda b, h: (b, first + h))
    return pl.pallas_call(
        _attn_kernel,
        grid=(bsz, nh),
        in_specs=[
            pl.BlockSpec(memory_space=pltpu.SMEM),
            pl.BlockSpec(memory_space=pltpu.SMEM),
            vec, vec, vec, vec,
            pl.BlockSpec((1, ATT_V_DIM), lambda b, h: (0, 0)),
            head(0), head(nh), head(2 * nh),
            head(0),
        ],
        out_specs=head(0),
        out_shape=jax.ShapeDtypeStruct((bsz * seq, nh * ATT_V_DIM), BF16),
        scratch_shapes=[
            pltpu.VMEM((seq, 2 * ATT_V_DIM), BF16),
            pltpu.VMEM((ATT_V_DIM + ATT_SUM_ROWS, seq), BF16),
        ] + [pltpu.VMEM(((c + 1) * blk, 2 * blk), F32) for c in range(n_blk)]
          + [pltpu.VMEM(((c + 1) * blk, 2 * blk), BF16) for c in range(n_blk)],
        compiler_params=pltpu.CompilerParams(
            dimension_semantics=("arbitrary", "arbitrary"), vmem_limit_bytes=VMEM_LIMIT),
        name="diff_attention",
    )(slopes, jnp.zeros((1,), jnp.int32), lq1, lk1, lq2, lk2, subln_g, qkv, qkv, qkv, za)


def _s5_disc_kernel(lre_ref, lim_ref, ldt_ref, bre_ref, bim_ref, are_ref, aim_ref, bbre_ref, bbim_ref):
    dt = jnp.exp(ldt_ref[...])
    lre = jnp.minimum(lre_ref[...], -1e-4)
    lim = lim_ref[...]
    mag = jnp.exp(lre * dt)
    lbar_re = mag * jnp.cos(lim * dt)
    lbar_im = mag * jnp.sin(lim * dt)
    num_re = lbar_re - 1.0
    den = lre * lre + lim * lim
    coef_re = (num_re * lre + lbar_im * lim) / den
    coef_im = (lbar_im * lre - num_re * lim) / den
    bre = bre_ref[...]
    bim = bim_ref[...]
    are_ref[...] = lbar_re
    aim_ref[...] = lbar_im
    bbre_ref[...] = coef_re * bre - coef_im * bim
    bbim_ref[...] = coef_re * bim + coef_im * bre


def _s5_discretise(lam_re, lam_im, log_dt, b_re, b_im):
    g, p = lam_re.shape
    hg = b_re.shape[-1]
    n = g * p
    row = lambda a: a.reshape(1, n)
    tr = lambda a: a.reshape(n, hg).T
    ldt = jnp.broadcast_to(log_dt[:, None], (g, p))
    return pl.pallas_call(
        _s5_disc_kernel,
        out_shape=[
            jax.ShapeDtypeStruct((1, n), F32), jax.ShapeDtypeStruct((1, n), F32),
            jax.ShapeDtypeStruct((hg, n), F32), jax.ShapeDtypeStruct((hg, n), F32),
        ],
        name="s5_discretise",
    )(row(lam_re), row(lam_im), row(ldt), tr(b_re), tr(b_im))


def _s5_kernel(u_ref, zs_ref, perm_ref, are_ref, aim_ref, wb_ref, wc_ref, d_ref, wglu_ref, bglu_ref, o_ref,
               x_scr, st_scr, *, bsz, steps, n_chunks):
    @pl.when(pl.program_id(0) == 0)
    def _():
        st_scr[...] = jnp.zeros_like(st_scr)

    cs = CHUNK_STATE
    perm = perm_ref[...]
    perm_steps = perm.shape[0] // bsz
    pm = bsz * perm_steps
    u = jnp.concatenate(
        [jnp.dot(perm, u_ref[:, t0:t0 + perm_steps, :].reshape(pm, u_ref.shape[2]),
                 preferred_element_type=F32) for t0 in range(0, steps, perm_steps)], axis=0).astype(BF16)

    def project_in(c):
        x_scr[:, 2 * cs * c:2 * cs * (c + 1)] = jnp.dot(
            u[:, CHUNK_IN * c:CHUNK_IN * (c + 1)], wb_ref[c], preferred_element_type=F32)

    def scan(c):
        re0 = 2 * cs * c
        im0 = re0 + cs
        a_re = jnp.broadcast_to(are_ref[:, cs * c:cs * (c + 1)], (bsz, cs))
        a_im = jnp.broadcast_to(aim_ref[:, cs * c:cs * (c + 1)], (bsz, cs))

        def step(t, carry):
            xr, xi = carry
            r0 = pl.multiple_of(t * bsz, bsz)
            nr = a_re * xr - a_im * xi + x_scr[pl.ds(r0, bsz), re0:re0 + cs]
            ni = a_re * xi + a_im * xr + x_scr[pl.ds(r0, bsz), im0:im0 + cs]
            x_scr[pl.ds(r0, bsz), re0:re0 + cs] = nr
            x_scr[pl.ds(r0, bsz), im0:im0 + cs] = ni
            return nr, ni

        xr, xi = lax.fori_loop(0, steps, step,
                               (st_scr[:, re0:re0 + cs], st_scr[:, im0:im0 + cs]), unroll=True)
        st_scr[:, re0:re0 + cs] = xr
        st_scr[:, im0:im0 + cs] = xi

    def project_out(c):
        return jnp.dot(x_scr[:, 2 * cs * c:2 * cs * (c + 1)].astype(BF16), wc_ref[c],
                       preferred_element_type=F32)

    ys = [None] * n_chunks
    project_in(0)
    for c in range(n_chunks):
        if c + 1 < n_chunks:
            project_in(c + 1)
        if c >= 1:
            ys[c - 1] = project_out(c - 1)
        scan(c)
    ys[n_chunks - 1] = project_out(n_chunks - 1)
    y = jnp.concatenate(ys, axis=1) + d_ref[...] * u.astype(F32)
    y = jax.nn.gelu(y)
    z = jnp.dot(y.astype(BF16), wglu_ref[...], preferred_element_type=F32) + bglu_ref[...]
    out = (y * _sigmoid(z)).astype(BF16)
    for i, t0 in enumerate(range(0, steps, perm_steps)):
        part = lax.dot_general(perm, out[i * pm:(i + 1) * pm, :], (((0,), (0,)), ((), ())),
                               preferred_element_type=F32)
        o_ref[:, t0:t0 + perm_steps, :] = (part.astype(BF16).reshape(bsz, perm_steps, part.shape[1])
                                           * zs_ref[:, t0:t0 + perm_steps, :])


def _s5(u_bt, zs_bt, lbar_re, lbar_im, wb, wc, d_row, w_glu, b_glu, bsz, seq):
    ssm_w = u_bt.shape[2]
    n_chunks = wb.shape[0]
    steps = S5_STEPS
    m = steps * bsz
    n_state = n_chunks * CHUNK_STATE
    kern = functools.partial(_s5_kernel, bsz=bsz, steps=steps, n_chunks=n_chunks)
    const2 = lambda i: (0, 0)
    const3 = lambda i: (0, 0, 0)
    rows = pl.BlockSpec((bsz, steps, ssm_w), lambda i: (0, i, 0))
    pm = bsz * S5_PERM_STEPS
    r = jnp.arange(pm)
    perm = (r[None, :] == ((r % bsz) * S5_PERM_STEPS + r // bsz)[:, None]).astype(BF16)
    return pl.pallas_call(
        kern,
        grid=(seq // steps,),
        in_specs=[
            rows,
            rows,
            pl.BlockSpec((pm, pm), const2),
            pl.BlockSpec((1, n_state), const2),
            pl.BlockSpec((1, n_state), const2),
            pl.BlockSpec(wb.shape, const3),
            pl.BlockSpec(wc.shape, const3),
            pl.BlockSpec((1, ssm_w), const2),
            pl.BlockSpec(w_glu.shape, const2),
            pl.BlockSpec((1, ssm_w), const2),
        ],
        out_specs=rows,
        out_shape=jax.ShapeDtypeStruct((bsz, seq, ssm_w), BF16),
        scratch_shapes=[
            pltpu.VMEM((m, 2 * n_state), F32),
            pltpu.VMEM((bsz, 2 * n_state), F32),
        ],
        compiler_params=pltpu.CompilerParams(
            dimension_semantics=("arbitrary",), vmem_limit_bytes=VMEM_LIMIT),
        name="s5_scan",
    )(u_bt, zs_bt, perm, lbar_re, lbar_im, wb, wc, d_row, w_glu, b_glu)


def _s5_block_diag_weights(bbar_re, bbar_im, c_re, c_im):
    hg, n = bbar_re.shape
    g = n // SSM_STATE
    nc = g // GROUPS_PER_CHUNK
    eye = jnp.eye(GROUPS_PER_CHUNK, dtype=F32)

    def wb_part(bb):
        t = bb.reshape(hg, nc, GROUPS_PER_CHUNK, SSM_STATE)
        w = jnp.einsum('ab,hcbp->cahbp', eye, t)
        return w.reshape(nc, GROUPS_PER_CHUNK * hg, GROUPS_PER_CHUNK * SSM_STATE)

    def wc_part(cc):
        t = cc.reshape(nc, GROUPS_PER_CHUNK, hg, SSM_STATE)
        w = jnp.einsum('ab,cbhp->capbh', eye, t)
        return w.reshape(nc, GROUPS_PER_CHUNK * SSM_STATE, GROUPS_PER_CHUNK * hg)

    wb = jnp.concatenate([wb_part(bbar_re), wb_part(bbar_im)], axis=2).astype(BF16)
    wc = jnp.concatenate([wc_part(c_re), -wc_part(c_im)], axis=1).astype(BF16)
    return wb, wc


def _out_kernel(att_ref, ssm_ref, gate_ref, x_ref, woa_ref, wos_ref, wout_ref, fg_ref, o_ref, *, d_model):
    ya = jnp.dot(att_ref[...], woa_ref[...], preferred_element_type=F32)
    ys = jnp.dot(ssm_ref[...], wos_ref[...], preferred_element_type=F32)
    ga = gate_ref[:, 0:d_model].astype(F32)
    gs = gate_ref[:, d_model:2 * d_model].astype(F32)
    merged = ga * ya + gs * ys
    o = x_ref[...] + jnp.dot(merged.astype(BF16), wout_ref[...], preferred_element_type=F32)
    ms = jnp.mean(o * o, axis=-1, keepdims=True)
    o_ref[...] = o * lax.rsqrt(ms + EPS) * fg_ref[...]


def _out_proj(att, ssm, gate, x2, w_o_att, w_o_ssm, w_out, final_g, bsz, seq):
    d_model = x2.shape[1]
    att_w = att.shape[1]
    ssm_w = w_o_ssm.shape[0]
    tm = ROW_TILE
    nl = seq // tm
    row = lambda b, l: (b * nl + l, 0)
    const = lambda b, l: (0, 0)
    kern = functools.partial(_out_kernel, d_model=d_model)
    return pl.pallas_call(
        kern,
        grid=(bsz, nl),
        in_specs=[
            pl.BlockSpec((tm, att_w), row),
            pl.BlockSpec((tm, ssm_w), row),
            pl.BlockSpec((tm, 2 * d_model), row),
            pl.BlockSpec((tm, d_model), row),
            pl.BlockSpec(w_o_att.shape, const),
            pl.BlockSpec(w_o_ssm.shape, const),
            pl.BlockSpec(w_out.shape, const),
            pl.BlockSpec((1, d_model), const),
        ],
        out_specs=pl.BlockSpec((tm, d_model), row),
        out_shape=jax.ShapeDtypeStruct((bsz * seq, d_model), F32),
        compiler_params=pltpu.CompilerParams(
            dimension_semantics=("arbitrary", "arbitrary"), vmem_limit_bytes=VMEM_LIMIT),
        name="out_proj",
    )(att, ssm, gate, x2, w_o_att, w_o_ssm, w_out, final_g)


def kernel(x, norm_g, w_in, lambda_q1, lambda_k1, lambda_q2, lambda_k2, subln_g, w_o_att,
           ssm_lambda_re, ssm_lambda_im, ssm_log_dt, ssm_b_re, ssm_b_im, ssm_c_re, ssm_c_im,
           ssm_d, w_glu, b_glu, w_o_ssm, w_out, final_g):
    bsz, seq, d_model = x.shape
    depth = w_in.shape[0]
    assert depth == 1, "single-layer block"
    att_w = w_o_att.shape[1]
    ssm_w = w_o_ssm.shape[1]
    assert att_w == ATT_HEADS * ATT_V_DIM
    assert w_in.shape[2] == 4 * att_w + 2 * ssm_w + 2 * d_model
    assert seq % ROW_TILE == 0 and seq % ATT_BLOCK == 0 and seq % S5_STEPS == 0
    assert S5_STEPS % S5_PERM_STEPS == 0
    assert bsz % 16 == 0, "batch rows fill whole bf16 sublane tiles in the S5 kernel"
    assert (ssm_w // SSM_GROUP) % GROUPS_PER_CHUNK == 0

    x2 = x.reshape(bsz * seq, d_model)
    qkv, za, u, zs, gate = _in_proj(x2, norm_g[0][None], w_in[0].astype(BF16), bsz, seq, att_w, ssm_w)

    slopes = jnp.asarray([2.0 ** (-8.0 * (h + 1) / ATT_HEADS) for h in range(ATT_HEADS)], F32)
    att = _attention(qkv, za, slopes, lambda_q1, lambda_k1, lambda_q2, lambda_k2, subln_g, bsz, seq)

    lbar_re, lbar_im, bbar_re, bbar_im = _s5_discretise(
        ssm_lambda_re[0], ssm_lambda_im[0], ssm_log_dt[0], ssm_b_re[0], ssm_b_im[0])
    wb, wc = _s5_block_diag_weights(bbar_re, bbar_im, ssm_c_re[0], ssm_c_im[0])
    ssm = _s5(u.reshape(bsz, seq, ssm_w), zs.reshape(bsz, seq, ssm_w), lbar_re, lbar_im, wb, wc,
              ssm_d[0].reshape(1, ssm_w), w_glu[0].astype(BF16), b_glu[0][None], bsz, seq)

    out = _out_proj(att, ssm.reshape(bsz * seq, ssm_w), gate, x2,
                    w_o_att[0].astype(BF16), w_o_ssm[0].astype(BF16), w_out[0].astype(BF16),
                    final_g[None], bsz, seq)
    return out.reshape(bsz, seq, d_model)
```

```python
import functools
import math

import jax
import jax.numpy as jnp
from jax import lax
from jax.experimental import pallas as pl
from jax.experimental.pallas import tpu as pltpu

EPS = 1e-5
ATT_HEADS = 8
ATT_HEAD_DIM = 64
ATT_V_DIM = 2 * ATT_HEAD_DIM
SSM_GROUP = 16
SSM_STATE = 64
LAMBDA_INIT = 0.8 - 0.6 * math.exp(-0.3 * 0)
LOG2E = math.log2(math.e)
Q_PRESCALE = ATT_HEAD_DIM ** -0.5 * LOG2E
BIAS_TERMS = 3

GROUPS_PER_CHUNK = 8
CHUNK_IN = GROUPS_PER_CHUNK * SSM_GROUP
CHUNK_STATE = GROUPS_PER_CHUNK * SSM_STATE

ROW_TILE = 1024
OUT_ROW_GROUPS = 4
ATT_BLOCK = 128
ATT_SUM_ROWS = 16
ATT_VALUE_LAG = 7
S5_STEPS = 64
S5_PERM_STEPS = 32
VMEM_LIMIT = 56 * 1024 * 1024

F32 = jnp.float32
BF16 = jnp.bfloat16


def _sigmoid(x):
    return 0.5 * jnp.tanh(0.5 * x) + 0.5


def _silu(x):
    return x * _sigmoid(x)


def _in_proj_kernel(x_ref, g_ref, w_ref, qkv_ref, za_ref, u_ref, zs_ref, gate_ref, *, att_w, ssm_w, d_model):
    x = x_ref[...]
    ms = jnp.mean(x * x, axis=-1, keepdims=True)
    h = (x * lax.rsqrt(ms + EPS) * g_ref[...]).astype(BF16)

    def proj(dst_ref, dst0, src0, width, fn=None, step=512):
        for o in range(0, width, step):
            r = jnp.dot(h, w_ref[:, src0 + o:src0 + o + step], preferred_element_type=F32)
            if fn is not None:
                r = fn(r)
            dst_ref[:, dst0 + o:dst0 + o + step] = r.astype(dst_ref.dtype)

    q0, z0 = 0, 3 * att_w
    u0 = z0 + att_w
    r0 = u0 + ssm_w
    proj(za_ref, 0, z0, att_w, fn=_silu)
    proj(zs_ref, 0, r0, ssm_w, fn=_silu)
    proj(gate_ref, 0, r0 + ssm_w, 2 * d_model, fn=_sigmoid)
    proj(qkv_ref, 0, q0, att_w, fn=lambda r: r * Q_PRESCALE)
    proj(u_ref, 0, u0, ssm_w)
    proj(qkv_ref, att_w, q0 + att_w, 2 * att_w)


def _in_proj(x2, norm_g, w_in, bsz, seq, att_w, ssm_w):
    d_model = x2.shape[1]
    tm = ROW_TILE
    nl = seq // tm
    kern = functools.partial(_in_proj_kernel, att_w=att_w, ssm_w=ssm_w, d_model=d_model)
    row = lambda b, l: (b * nl + l, 0)
    return pl.pallas_call(
        kern,
        grid=(bsz, nl),
        in_specs=[
            pl.BlockSpec((tm, d_model), row),
            pl.BlockSpec((1, d_model), lambda b, l: (0, 0)),
            pl.BlockSpec(w_in.shape, lambda b, l: (0, 0), pipeline_mode=pl.Buffered(1)),
        ],
        out_specs=[
            pl.BlockSpec((tm, 3 * att_w), row),
            pl.BlockSpec((tm, att_w), row),
            pl.BlockSpec((tm, ssm_w), row),
            pl.BlockSpec((tm, ssm_w), row),
            pl.BlockSpec((tm, 2 * d_model), row),
        ],
        out_shape=[
            jax.ShapeDtypeStruct((bsz * seq, 3 * att_w), BF16),
            jax.ShapeDtypeStruct((bsz * seq, att_w), BF16),
            jax.ShapeDtypeStruct((bsz * seq, ssm_w), BF16),
            jax.ShapeDtypeStruct((bsz * seq, ssm_w), BF16),
            jax.ShapeDtypeStruct((bsz * seq, 2 * d_model), BF16),
        ],
        compiler_params=pltpu.CompilerParams(
            dimension_semantics=("arbitrary", "arbitrary"), vmem_limit_bytes=VMEM_LIMIT),
        name="in_proj",
    )(x2, norm_g, w_in)


def _split_bf16(v, n):
    parts = []
    for _ in range(n):
        p = v.astype(BF16)
        parts.append(p)
        v = v - p.astype(F32)
    return parts


def _attn_kernel(slopes_ref, zero_ref, lq1_ref, lk1_ref, lq2_ref, lk2_ref, sg_ref, q_ref, k_ref, v_ref,
                 za_ref, o_ref, kx_ref, vt_ref, *sp_refs):
    blk = ATT_BLOCK
    hd = ATT_HEAD_DIM
    vd = ATT_V_DIM
    n_blk = k_ref.shape[0] // blk
    s_refs, p_refs = sp_refs[:n_blk], sp_refs[n_blk:]

    slope2 = slopes_ref[pl.program_id(1)] * LOG2E
    lane = lax.broadcasted_iota(jnp.int32, (blk, vd), 1)
    pos = lax.broadcasted_iota(jnp.int32, (blk, vd), 0)
    local = jnp.zeros((blk, vd), F32)
    for t, p in enumerate(_split_bf16(slope2 * pos.astype(F32), BIAS_TERMS)):
        local = jnp.where(lane == t, p.astype(F32), local)
    lane_row = lax.broadcasted_iota(jnp.int32, (1, vd), 1)
    for j in range(n_blk):
        rows = pl.ds(j * blk, blk)
        offset = jnp.zeros((1, vd), F32)
        for t, p in enumerate(_split_bf16(slope2 * jnp.full((1, vd), j * blk, F32), BIAS_TERMS)):
            offset = jnp.where(lane_row == BIAS_TERMS + t, p.astype(F32), offset)
        kx_ref[rows, 0:vd] = k_ref[rows, :]
        kx_ref[rows, vd:2 * vd] = (local + offset).astype(BF16)
        vt_ref[0:vd, j * blk:(j + 1) * blk] = v_ref[rows, :].astype(F32).T.astype(BF16)
    sub = lax.broadcasted_iota(jnp.int32, (ATT_SUM_ROWS, vt_ref.shape[1]), 0)
    vt_ref[vd:vd + ATT_SUM_ROWS, :] = jnp.where(sub == 0, 1.0, 0.0).astype(BF16)

    e1 = jnp.exp(jnp.sum(lq1_ref[...] * lk1_ref[...], axis=-1, keepdims=True))
    e2 = jnp.exp(jnp.sum(lq2_ref[...] * lk2_ref[...], axis=-1, keepdims=True))
    lam = e1 - e2 + LAMBDA_INIT
    gain = sg_ref[...] * (1.0 - LAMBDA_INIT)
    nt = (((1,), (1,)), ((), ()))

    key = lax.broadcasted_iota(jnp.int32, (blk, 2 * blk), 0)
    qry = lax.broadcasted_iota(jnp.int32, (blk, 2 * blk), 1)
    causal = key <= jnp.where(qry >= blk, qry - blk, qry)

    dyn0 = zero_ref[0]
    col_max = {}

    def score_stage(c):
        nk = (c + 1) * blk
        q = q_ref[c * blk:(c + 1) * blk, :].astype(F32)
        zero = jnp.zeros_like(q)
        ones = jnp.where(lane < 2 * BIAS_TERMS, jnp.ones_like(q), zero)
        qx = jnp.concatenate([
            jnp.concatenate([jnp.where(lane < hd, q, zero), ones], axis=1),
            jnp.concatenate([jnp.where(lane >= hd, q, zero), ones], axis=1)], axis=0).astype(BF16)
        s = lax.dot_general(kx_ref[0:nk, :], qx, nt, preferred_element_type=F32)
        m = None
        for j in range(c + 1):
            sj = s[j * blk:(j + 1) * blk, :]
            if j == c:
                sj = jnp.where(causal, sj, -jnp.inf)
            s_refs[c][j * blk:(j + 1) * blk, :] = sj
            mj = jnp.max(sj, axis=0, keepdims=True)
            m = mj if m is None else jnp.maximum(m, mj)
        col_max[c] = m

    def softmax_stage(c):
        for j in range(c + 1):
            sj = s_refs[c][pl.ds(pl.multiple_of(dyn0 + j * blk, blk), blk), :]
            p_refs[c][j * blk:(j + 1) * blk, :] = jnp.exp2(sj - col_max[c]).astype(BF16)

    def value_stage(c):
        nk = (c + 1) * blk
        p = p_refs[c][pl.ds(pl.multiple_of(dyn0, blk), nk), :]
        acc = jnp.dot(vt_ref[:, 0:nk], p, preferred_element_type=F32)
        r = 1.0 / acc[vd:vd + 1, :]
        ot = acc[0:vd, 0:blk] * r[:, 0:blk] - acc[0:vd, blk:2 * blk] * (lam * r[:, blk:2 * blk])
        ms = jnp.mean(ot * ot, axis=0, keepdims=True)
        att = (ot * lax.rsqrt(ms + EPS)).T * gain
        o_ref[c * blk:(c + 1) * blk, :] = att.astype(BF16) * za_ref[c * blk:(c + 1) * blk, :]

    order = list(reversed(range(n_blk)))
    for t in range(n_blk + ATT_VALUE_LAG):
        if t < n_blk:
            score_stage(order[t])
            softmax_stage(order[t])
        if t >= ATT_VALUE_LAG:
            value_stage(order[t - ATT_VALUE_LAG])


def _attention(qkv, za, slopes, lq1, lk1, lq2, lk2, subln_g, bsz, seq):
    blk = ATT_BLOCK
    n_blk = seq // blk
    nh = ATT_HEADS
    vec = pl.BlockSpec((1, ATT_HEAD_DIM), lambda b, h: (0, 0))
    head = lambda first: pl.BlockSpec((seq, ATT_V_DIM), lambda b, h: (b, first + h))
    return pl.pallas_call(
        _attn_kernel,
        grid=(bsz, nh),
        in_specs=[
            pl.BlockSpec(memory_space=pltpu.SMEM),
            pl.BlockSpec(memory_space=pltpu.SMEM),
            vec, vec, vec, vec,
            pl.BlockSpec((1, ATT_V_DIM), lambda b, h: (0, 0)),
            head(0), head(nh), head(2 * nh),
            head(0),
        ],
        out_specs=head(0),
        out_shape=jax.ShapeDtypeStruct((bsz * seq, nh * ATT_V_DIM), BF16),
        scratch_shapes=[
            pltpu.VMEM((seq, 2 * ATT_V_DIM), BF16),
            pltpu.VMEM((ATT_V_DIM + ATT_SUM_ROWS, seq), BF16),
        ] + [pltpu.VMEM(((c + 1) * blk, 2 * blk), F32) for c in range(n_blk)]
          + [pltpu.VMEM(((c + 1) * blk, 2 * blk), BF16) for c in range(n_blk)],
        compiler_params=pltpu.CompilerParams(
            dimension_semantics=("arbitrary", "arbitrary"), vmem_limit_bytes=VMEM_LIMIT),
        name="diff_attention",
    )(slopes, jnp.zeros((1,), jnp.int32), lq1, lk1, lq2, lk2, subln_g, qkv, qkv, qkv, za)


def _s5_disc_kernel(lre_ref, lim_ref, ldt_ref, bre_ref, bim_ref, are_ref, aim_ref, bbre_ref, bbim_ref):
    dt = jnp.exp(ldt_ref[...])
    lre = jnp.minimum(lre_ref[...], -1e-4)
    lim = lim_ref[...]
    mag = jnp.exp(lre * dt)
    lbar_re = mag * jnp.cos(lim * dt)
    lbar_im = mag * jnp.sin(lim * dt)
    num_re = lbar_re - 1.0
    den = lre * lre + lim * lim
    coef_re = (num_re * lre + lbar_im * lim) / den
    coef_im = (lbar_im * lre - num_re * lim) / den
    bre = bre_ref[...]
    bim = bim_ref[...]
    are_ref[...] = lbar_re
    aim_ref[...] = lbar_im
    bbre_ref[...] = coef_re * bre - coef_im * bim
    bbim_ref[...] = coef_re * bim + coef_im * bre


def _s5_discretise(lam_re, lam_im, log_dt, b_re, b_im):
    g, p = lam_re.shape
    hg = b_re.shape[-1]
    n = g * p
    row = lambda a: a.reshape(1, n)
    tr = lambda a: a.reshape(n, hg).T
    ldt = jnp.broadcast_to(log_dt[:, None], (g, p))
    return pl.pallas_call(
        _s5_disc_kernel,
        out_shape=[
            jax.ShapeDtypeStruct((1, n), F32), jax.ShapeDtypeStruct((1, n), F32),
            jax.ShapeDtypeStruct((hg, n), F32), jax.ShapeDtypeStruct((hg, n), F32),
        ],
        name="s5_discretise",
    )(row(lam_re), row(lam_im), row(ldt), tr(b_re), tr(b_im))


def _s5_kernel(u_ref, zs_ref, perm_ref, are_ref, aim_ref, wb_ref, wc_ref, d_ref, wglu_ref, bglu_ref, o_ref,
               x_scr, st_scr, *, bsz, steps, n_chunks):
    @pl.when(pl.program_id(0) == 0)
    def _():
        st_scr[...] = jnp.zeros_like(st_scr)

    cs = CHUNK_STATE
    perm = perm_ref[...]
    perm_steps = perm.shape[0] // bsz
    pm = bsz * perm_steps
    u = jnp.concatenate(
        [jnp.dot(perm, u_ref[:, t0:t0 + perm_steps, :].reshape(pm, u_ref.shape[2]),
                 preferred_element_type=F32) for t0 in range(0, steps, perm_steps)], axis=0).astype(BF16)

    def project_in(c):
        x_scr[:, 2 * cs * c:2 * cs * (c + 1)] = jnp.dot(
            u[:, CHUNK_IN * c:CHUNK_IN * (c + 1)], wb_ref[c], preferred_element_type=F32)

    def scan(c):
        re0 = 2 * cs * c
        im0 = re0 + cs
        a_re = jnp.broadcast_to(are_ref[:, cs * c:cs * (c + 1)], (bsz, cs))
        a_im = jnp.broadcast_to(aim_ref[:, cs * c:cs * (c + 1)], (bsz, cs))

        def step(t, carry):
            xr, xi = carry
            r0 = pl.multiple_of(t * bsz, bsz)
            nr = a_re * xr - a_im * xi + x_scr[pl.ds(r0, bsz), re0:re0 + cs]
            ni = a_re * xi + a_im * xr + x_scr[pl.ds(r0, bsz), im0:im0 + cs]
            x_scr[pl.ds(r0, bsz), re0:re0 + cs] = nr
            x_scr[pl.ds(r0, bsz), im0:im0 + cs] = ni
            return nr, ni

        xr, xi = lax.fori_loop(0, steps, step,
                               (st_scr[:, re0:re0 + cs], st_scr[:, im0:im0 + cs]), unroll=True)
        st_scr[:, re0:re0 + cs] = xr
        st_scr[:, im0:im0 + cs] = xi

    def project_out(c):
        return jnp.dot(x_scr[:, 2 * cs * c:2 * cs * (c + 1)].astype(BF16), wc_ref[c],
                       preferred_element_type=F32)

    for c in range(n_chunks):
        project_in(c)
    for c in range(n_chunks):
        scan(c)
    ys = [project_out(c) for c in range(n_chunks)]
    y = jnp.concatenate(ys, axis=1) + d_ref[...] * u.astype(F32)
    y = jax.nn.gelu(y)
    z = jnp.dot(y.astype(BF16), wglu_ref[...], preferred_element_type=F32) + bglu_ref[...]
    out = (y * _sigmoid(z)).astype(BF16)
    for i, t0 in enumerate(range(0, steps, perm_steps)):
        part = lax.dot_general(perm, out[i * pm:(i + 1) * pm, :], (((0,), (0,)), ((), ())),
                               preferred_element_type=F32)
        o_ref[:, t0:t0 + perm_steps, :] = (part.astype(BF16).reshape(bsz, perm_steps, part.shape[1])
                                           * zs_ref[:, t0:t0 + perm_steps, :])


def _s5(u_bt, zs_bt, lbar_re, lbar_im, wb, wc, d_row, w_glu, b_glu, bsz, seq):
    ssm_w = u_bt.shape[2]
    n_chunks = wb.shape[0]
    steps = S5_STEPS
    m = steps * bsz
    n_state = n_chunks * CHUNK_STATE
    kern = functools.partial(_s5_kernel, bsz=bsz, steps=steps, n_chunks=n_chunks)
    const2 = lambda i: (0, 0)
    const3 = lambda i: (0, 0, 0)
    rows = pl.BlockSpec((bsz, steps, ssm_w), lambda i: (0, i, 0))
    pm = bsz * S5_PERM_STEPS
    r = jnp.arange(pm)
    perm = (r[None, :] == ((r % bsz) * S5_PERM_STEPS + r // bsz)[:, None]).astype(BF16)
    return pl.pallas_call(
        kern,
        grid=(seq // steps,),
        in_specs=[
            rows,
            rows,
            pl.BlockSpec((pm, pm), const2),
            pl.BlockSpec((1, n_state), const2),
            pl.BlockSpec((1, n_state), const2),
            pl.BlockSpec(wb.shape, const3),
            pl.BlockSpec(wc.shape, const3),
            pl.BlockSpec((1, ssm_w), const2),
            pl.BlockSpec(w_glu.shape, const2),
            pl.BlockSpec((1, ssm_w), const2),
        ],
        out_specs=rows,
        out_shape=jax.ShapeDtypeStruct((bsz, seq, ssm_w), BF16),
        scratch_shapes=[
            pltpu.VMEM((m, 2 * n_state), F32),
            pltpu.VMEM((bsz, 2 * n_state), F32),
        ],
        compiler_params=pltpu.CompilerParams(
            dimension_semantics=("arbitrary",), vmem_limit_bytes=VMEM_LIMIT),
        name="s5_scan",
    )(u_bt, zs_bt, perm, lbar_re, lbar_im, wb, wc, d_row, w_glu, b_glu)


def _s5_block_diag_weights(bbar_re, bbar_im, c_re, c_im):
    hg, n = bbar_re.shape
    g = n // SSM_STATE
    nc = g // GROUPS_PER_CHUNK
    eye = jnp.eye(GROUPS_PER_CHUNK, dtype=F32)

    def wb_part(bb):
        t = bb.reshape(hg, nc, GROUPS_PER_CHUNK, SSM_STATE)
        w = jnp.einsum('ab,hcbp->cahbp', eye, t)
        return w.reshape(nc, GROUPS_PER_CHUNK * hg, GROUPS_PER_CHUNK * SSM_STATE)

    def wc_part(cc):
        t = cc.reshape(nc, GROUPS_PER_CHUNK, hg, SSM_STATE)
        w = jnp.einsum('ab,cbhp->capbh', eye, t)
        return w.reshape(nc, GROUPS_PER_CHUNK * SSM_STATE, GROUPS_PER_CHUNK * hg)

    wb = jnp.concatenate([wb_part(bbar_re), wb_part(bbar_im)], axis=2).astype(BF16)
    wc = jnp.concatenate([wc_part(c_re), -wc_part(c_im)], axis=1).astype(BF16)
    return wb, wc


def _out_kernel(att_ref, ssm_ref, gate_ref, x_ref, woa_ref, wos_ref, wout_ref, fg_ref, o_ref, *, d_model):
    rows_per = att_ref.shape[0] // OUT_ROW_GROUPS
    groups = [slice(i * rows_per, (i + 1) * rows_per) for i in range(OUT_ROW_GROUPS)]
    merged = []
    for rows in groups:
        ya = jnp.dot(att_ref[rows, :], woa_ref[...], preferred_element_type=F32)
        ys = jnp.dot(ssm_ref[rows, :], wos_ref[...], preferred_element_type=F32)
        ga = gate_ref[rows, 0:d_model].astype(F32)
        gs = gate_ref[rows, d_model:2 * d_model].astype(F32)
        merged.append((ga * ya + gs * ys).astype(BF16))
    for rows, mg in zip(groups, merged):
        o = x_ref[rows, :] + jnp.dot(mg, wout_ref[...], preferred_element_type=F32)
        ms = jnp.mean(o * o, axis=-1, keepdims=True)
        o_ref[rows, :] = o * lax.rsqrt(ms + EPS) * fg_ref[...]


def _out_proj(att, ssm, gate, x2, w_o_att, w_o_ssm, w_out, final_g, bsz, seq):
    d_model = x2.shape[1]
    att_w = att.shape[1]
    ssm_w = w_o_ssm.shape[0]
    tm = ROW_TILE
    nl = seq // tm
    row = lambda b, l: (b * nl + l, 0)
    const = lambda b, l: (0, 0)
    kern = functools.partial(_out_kernel, d_model=d_model)
    return pl.pallas_call(
        kern,
        grid=(bsz, nl),
        in_specs=[
            pl.BlockSpec((tm, att_w), row),
            pl.BlockSpec((tm, ssm_w), row),
            pl.BlockSpec((tm, 2 * d_model), row),
            pl.BlockSpec((tm, d_model), row),
            pl.BlockSpec(w_o_att.shape, const),
            pl.BlockSpec(w_o_ssm.shape, const),
            pl.BlockSpec(w_out.shape, const),
            pl.BlockSpec((1, d_model), const),
        ],
        out_specs=pl.BlockSpec((tm, d_model), row),
        out_shape=jax.ShapeDtypeStruct((bsz * seq, d_model), F32),
        compiler_params=pltpu.CompilerParams(
            dimension_semantics=("arbitrary", "arbitrary"), vmem_limit_bytes=VMEM_LIMIT),
        name="out_proj",
    )(att, ssm, gate, x2, w_o_att, w_o_ssm, w_out, final_g)


def kernel(x, norm_g, w_in, lambda_q1, lambda_k1, lambda_q2, lambda_k2, subln_g, w_o_att,
           ssm_lambda_re, ssm_lambda_im, ssm_log_dt, ssm_b_re, ssm_b_im, ssm_c_re, ssm_c_im,
           ssm_d, w_glu, b_glu, w_o_ssm, w_out, final_g):
    bsz, seq, d_model = x.shape
    depth = w_in.shape[0]
    assert depth == 1, "single-layer block"
    att_w = w_o_att.shape[1]
    ssm_w = w_o_ssm.shape[1]
    assert att_w == ATT_HEADS * ATT_V_DIM
    assert w_in.shape[2] == 4 * att_w + 2 * ssm_w + 2 * d_model
    assert seq % ROW_TILE == 0 and seq % ATT_BLOCK == 0 and seq % S5_STEPS == 0
    assert S5_STEPS % S5_PERM_STEPS == 0
    assert bsz % 16 == 0, "batch rows fill whole bf16 sublane tiles in the S5 kernel"
    assert (ssm_w // SSM_GROUP) % GROUPS_PER_CHUNK == 0

    x2 = x.reshape(bsz * seq, d_model)
    qkv, za, u, zs, gate = _in_proj(x2, norm_g[0][None], w_in[0].astype(BF16), bsz, seq, att_w, ssm_w)

    slopes = jnp.asarray([2.0 ** (-8.0 * (h + 1) / ATT_HEADS) for h in range(ATT_HEADS)], F32)
    att = _attention(qkv, za, slopes, lambda_q1, lambda_k1, lambda_q2, lambda_k2, subln_g, bsz, seq)

    lbar_re, lbar_im, bbar_re, bbar_im = _s5_discretise(
        ssm_lambda_re[0], ssm_lambda_im[0], ssm_log_dt[0], ssm_b_re[0], ssm_b_im[0])
    wb, wc = _s5_block_diag_weights(bbar_re, bbar_im, ssm_c_re[0], ssm_c_im[0])
    ssm = _s5(u.reshape(bsz, seq, ssm_w), zs.reshape(bsz, seq, ssm_w), lbar_re, lbar_im, wb, wc,
              ssm_d[0].reshape(1, ssm_w), w_glu[0].astype(BF16), b_glu[0][None], bsz, seq)

    out = _out_proj(att, ssm.reshape(bsz * seq, ssm_w), gate, x2,
                    w_o_att[0].astype(BF16), w_o_ssm[0].astype(BF16), w_out[0].astype(BF16),
                    final_g[None], bsz, seq)
    return out.reshape(bsz, seq, d_model)
```

```python
import functools
import math

import jax
import jax.numpy as jnp
from jax import lax
from jax.experimental import pallas as pl
from jax.experimental.pallas import tpu as pltpu

EPS = 1e-5
ATT_HEADS = 8
ATT_HEAD_DIM = 64
ATT_V_DIM = 2 * ATT_HEAD_DIM
SSM_GROUP = 16
SSM_STATE = 64
LAMBDA_INIT = 0.8 - 0.6 * math.exp(-0.3 * 0)
LOG2E = math.log2(math.e)
Q_PRESCALE = ATT_HEAD_DIM ** -0.5 * LOG2E
BIAS_TERMS = 3

GROUPS_PER_CHUNK = 8
CHUNK_IN = GROUPS_PER_CHUNK * SSM_GROUP
CHUNK_STATE = GROUPS_PER_CHUNK * SSM_STATE

ROW_TILE = 1024
ATT_BLOCK = 128
ATT_SUM_ROWS = 16
ATT_VALUE_LAG = 7
S5_STEPS = 64
S5_PERM_STEPS = 32
VMEM_LIMIT = 56 * 1024 * 1024

F32 = jnp.float32
BF16 = jnp.bfloat16


def _sigmoid(x):
    return 0.5 * jnp.tanh(0.5 * x) + 0.5


def _silu(x):
    return x * _sigmoid(x)


def _in_proj_kernel(x_ref, g_ref, w_ref, qkv_ref, za_ref, u_ref, zs_ref, gate_ref, *, att_w, ssm_w, d_model):
    x = x_ref[...]
    ms = jnp.mean(x * x, axis=-1, keepdims=True)
    h = (x * lax.rsqrt(ms + EPS) * g_ref[...]).astype(BF16)

    def proj(dst_ref, dst0, src0, width, fn=None, step=512):
        for o in range(0, width, step):
            r = jnp.dot(h, w_ref[:, src0 + o:src0 + o + step], preferred_element_type=F32)
            if fn is not None:
                r = fn(r)
            r = r.astype(dst_ref.dtype)
            if len(dst_ref.shape) == 3:
                for i in range(step // ATT_V_DIM):
                    dst_ref[(dst0 + o) // ATT_V_DIM + i] = r[:, i * ATT_V_DIM:(i + 1) * ATT_V_DIM]
            else:
                dst_ref[:, dst0 + o:dst0 + o + step] = r

    q0, z0 = 0, 3 * att_w
    u0 = z0 + att_w
    r0 = u0 + ssm_w
    proj(za_ref, 0, z0, att_w, fn=_silu)
    proj(zs_ref, 0, r0, ssm_w, fn=_silu)
    proj(gate_ref, 0, r0 + ssm_w, 2 * d_model, fn=_sigmoid)
    proj(qkv_ref, 0, q0, att_w, fn=lambda r: r * Q_PRESCALE)
    proj(u_ref, 0, u0, ssm_w)
    proj(qkv_ref, att_w, q0 + att_w, 2 * att_w)


def _in_proj(x2, norm_g, w_in, bsz, seq, att_w, ssm_w):
    d_model = x2.shape[1]
    tm = ROW_TILE
    nl = seq // tm
    kern = functools.partial(_in_proj_kernel, att_w=att_w, ssm_w=ssm_w, d_model=d_model)
    row = lambda b, l: (b * nl + l, 0)
    n_qkv = 3 * att_w // ATT_V_DIM
    heads_of = lambda n: pl.BlockSpec((None, n, tm, ATT_V_DIM), lambda b, l: (b, 0, l, 0))
    return pl.pallas_call(
        kern,
        grid=(bsz, nl),
        in_specs=[
            pl.BlockSpec((tm, d_model), row),
            pl.BlockSpec((1, d_model), lambda b, l: (0, 0)),
            pl.BlockSpec(w_in.shape, lambda b, l: (0, 0), pipeline_mode=pl.Buffered(1)),
        ],
        out_specs=[
            heads_of(n_qkv),
            heads_of(ATT_HEADS),
            pl.BlockSpec((tm, ssm_w), row),
            pl.BlockSpec((tm, ssm_w), row),
            pl.BlockSpec((tm, 2 * d_model), row),
        ],
        out_shape=[
            jax.ShapeDtypeStruct((bsz, n_qkv, seq, ATT_V_DIM), BF16),
            jax.ShapeDtypeStruct((bsz, ATT_HEADS, seq, ATT_V_DIM), BF16),
            jax.ShapeDtypeStruct((bsz * seq, ssm_w), BF16),
            jax.ShapeDtypeStruct((bsz * seq, ssm_w), BF16),
            jax.ShapeDtypeStruct((bsz * seq, 2 * d_model), BF16),
        ],
        compiler_params=pltpu.CompilerParams(
            dimension_semantics=("arbitrary", "arbitrary"), vmem_limit_bytes=VMEM_LIMIT),
        name="in_proj",
    )(x2, norm_g, w_in)


def _split_bf16(v, n):
    parts = []
    for _ in range(n):
        p = v.astype(BF16)
        parts.append(p)
        v = v - p.astype(F32)
    return parts


def _attn_kernel(slopes_ref, zero_ref, lq1_ref, lk1_ref, lq2_ref, lk2_ref, sg_ref, q_ref, k_ref, v_ref,
                 za_ref, o_ref, kx_ref, vt_ref, *sp_refs):
    blk = ATT_BLOCK
    hd = ATT_HEAD_DIM
    vd = ATT_V_DIM
    n_blk = k_ref.shape[0] // blk
    s_refs, p_refs = sp_refs[:n_blk], sp_refs[n_blk:]

    slope2 = slopes_ref[pl.program_id(1)] * LOG2E
    lane = lax.broadcasted_iota(jnp.int32, (blk, vd), 1)
    pos = lax.broadcasted_iota(jnp.int32, (blk, vd), 0)
    local = jnp.zeros((blk, vd), F32)
    for t, p in enumerate(_split_bf16(slope2 * pos.astype(F32), BIAS_TERMS)):
        local = jnp.where(lane == t, p.astype(F32), local)
    lane_row = lax.broadcasted_iota(jnp.int32, (1, vd), 1)
    for j in range(n_blk):
        rows = pl.ds(j * blk, blk)
        offset = jnp.zeros((1, vd), F32)
        for t, p in enumerate(_split_bf16(slope2 * jnp.full((1, vd), j * blk, F32), BIAS_TERMS)):
            offset = jnp.where(lane_row == BIAS_TERMS + t, p.astype(F32), offset)
        kx_ref[rows, 0:vd] = k_ref[rows, :]
        kx_ref[rows, vd:2 * vd] = (local + offset).astype(BF16)
        vt_ref[0:vd, j * blk:(j + 1) * blk] = v_ref[rows, :].astype(F32).T.astype(BF16)
    sub = lax.broadcasted_iota(jnp.int32, (ATT_SUM_ROWS, vt_ref.shape[1]), 0)
    vt_ref[vd:vd + ATT_SUM_ROWS, :] = jnp.where(sub == 0, 1.0, 0.0).astype(BF16)

    e1 = jnp.exp(jnp.sum(lq1_ref[...] * lk1_ref[...], axis=-1, keepdims=True))
    e2 = jnp.exp(jnp.sum(lq2_ref[...] * lk2_ref[...], axis=-1, keepdims=True))
    lam = e1 - e2 + LAMBDA_INIT
    gain = sg_ref[...] * (1.0 - LAMBDA_INIT)
    nt = (((1,), (1,)), ((), ()))

    key = lax.broadcasted_iota(jnp.int32, (blk, 2 * blk), 0)
    qry = lax.broadcasted_iota(jnp.int32, (blk, 2 * blk), 1)
    causal = key <= jnp.where(qry >= blk, qry - blk, qry)

    dyn0 = zero_ref[0]
    col_max = {}

    def score_stage(c):
        nk = (c + 1) * blk
        q = q_ref[c * blk:(c + 1) * blk, :].astype(F32)
        zero = jnp.zeros_like(q)
        ones = jnp.where(lane < 2 * BIAS_TERMS, jnp.ones_like(q), zero)
        qx = jnp.concatenate([
            jnp.concatenate([jnp.where(lane < hd, q, zero), ones], axis=1),
            jnp.concatenate([jnp.where(lane >= hd, q, zero), ones], axis=1)], axis=0).astype(BF16)
        s = lax.dot_general(kx_ref[0:nk, :], qx, nt, preferred_element_type=F32)
        m = None
        for j in range(c + 1):
            sj = s[j * blk:(j + 1) * blk, :]
            if j == c:
                sj = jnp.where(causal, sj, -jnp.inf)
            s_refs[c][j * blk:(j + 1) * blk, :] = sj
            mj = jnp.max(sj, axis=0, keepdims=True)
            m = mj if m is None else jnp.maximum(m, mj)
        col_max[c] = m

    def softmax_stage(c):
        for j in range(c + 1):
            sj = s_refs[c][pl.ds(pl.multiple_of(dyn0 + j * blk, blk), blk), :]
            p_refs[c][j * blk:(j + 1) * blk, :] = jnp.exp2(sj - col_max[c]).astype(BF16)

    def value_stage(c):
        nk = (c + 1) * blk
        p = p_refs[c][pl.ds(pl.multiple_of(dyn0, blk), nk), :]
        acc = jnp.dot(vt_ref[:, 0:nk], p, preferred_element_type=F32)
        r = 1.0 / acc[vd:vd + 1, :]
        ot = acc[0:vd, 0:blk] * r[:, 0:blk] - acc[0:vd, blk:2 * blk] * (lam * r[:, blk:2 * blk])
        ms = jnp.mean(ot * ot, axis=0, keepdims=True)
        att = (ot * lax.rsqrt(ms + EPS)).T * gain
        o_ref[c * blk:(c + 1) * blk, :] = att.astype(BF16) * za_ref[c * blk:(c + 1) * blk, :]

    order = list(reversed(range(n_blk)))
    for t in range(n_blk + ATT_VALUE_LAG):
        if t < n_blk:
            score_stage(order[t])
            softmax_stage(order[t])
        if t >= ATT_VALUE_LAG:
            value_stage(order[t - ATT_VALUE_LAG])


def _attention(qkv, za, slopes, lq1, lk1, lq2, lk2, subln_g, bsz, seq):
    blk = ATT_BLOCK
    n_blk = seq // blk
    nh = ATT_HEADS
    vec = pl.BlockSpec((1, ATT_HEAD_DIM), lambda b, h: (0, 0))
    head = lambda first: pl.BlockSpec((None, None, seq, ATT_V_DIM), lambda b, h: (b, first + h, 0, 0))
    return pl.pallas_call(
        _attn_kernel,
        grid=(bsz, nh),
        in_specs=[
            pl.BlockSpec(memory_space=pltpu.SMEM),
            pl.BlockSpec(memory_space=pltpu.SMEM),
            vec, vec, vec, vec,
            pl.BlockSpec((1, ATT_V_DIM), lambda b, h: (0, 0)),
            head(0), head(nh), head(2 * nh),
            head(0),
        ],
        out_specs=head(0),
        out_shape=jax.ShapeDtypeStruct((bsz, nh, seq, ATT_V_DIM), BF16),
        scratch_shapes=[
            pltpu.VMEM((seq, 2 * ATT_V_DIM), BF16),
            pltpu.VMEM((ATT_V_DIM + ATT_SUM_ROWS, seq), BF16),
        ] + [pltpu.VMEM(((c + 1) * blk, 2 * blk), F32) for c in range(n_blk)]
          + [pltpu.VMEM(((c + 1) * blk, 2 * blk), BF16) for c in range(n_blk)],
        compiler_params=pltpu.CompilerParams(
            dimension_semantics=("arbitrary", "arbitrary"), vmem_limit_bytes=VMEM_LIMIT),
        name="diff_attention",
    )(slopes, jnp.zeros((1,), jnp.int32), lq1, lk1, lq2, lk2, subln_g, qkv, qkv, qkv, za)


def _s5_disc_kernel(lre_ref, lim_ref, ldt_ref, bre_ref, bim_ref, are_ref, aim_ref, bbre_ref, bbim_ref):
    dt = jnp.exp(ldt_ref[...])
    lre = jnp.minimum(lre_ref[...], -1e-4)
    lim = lim_ref[...]
    mag = jnp.exp(lre * dt)
    lbar_re = mag * jnp.cos(lim * dt)
    lbar_im = mag * jnp.sin(lim * dt)
    num_re = lbar_re - 1.0
    den = lre * lre + lim * lim
    coef_re = (num_re * lre + lbar_im * lim) / den
    coef_im = (lbar_im * lre - num_re * lim) / den
    bre = bre_ref[...]
    bim = bim_ref[...]
    are_ref[...] = lbar_re
    aim_ref[...] = lbar_im
    bbre_ref[...] = coef_re * bre - coef_im * bim
    bbim_ref[...] = coef_re * bim + coef_im * bre


def _s5_discretise(lam_re, lam_im, log_dt, b_re, b_im):
    g, p = lam_re.shape
    hg = b_re.shape[-1]
    n = g * p
    row = lambda a: a.reshape(1, n)
    tr = lambda a: a.reshape(n, hg).T
    ldt = jnp.broadcast_to(log_dt[:, None], (g, p))
    return pl.pallas_call(
        _s5_disc_kernel,
        out_shape=[
            jax.ShapeDtypeStruct((1, n), F32), jax.ShapeDtypeStruct((1, n), F32),
            jax.ShapeDtypeStruct((hg, n), F32), jax.ShapeDtypeStruct((hg, n), F32),
        ],
        name="s5_discretise",
    )(row(lam_re), row(lam_im), row(ldt), tr(b_re), tr(b_im))


def _s5_kernel(u_ref, zs_ref, perm_ref, are_ref, aim_ref, wb_ref, wc_ref, d_ref, wglu_ref, bglu_ref, o_ref,
               x_scr, st_scr, *, bsz, steps, n_chunks):
    @pl.when(pl.program_id(0) == 0)
    def _():
        st_scr[...] = jnp.zeros_like(st_scr)

    cs = CHUNK_STATE
    perm = perm_ref[...]
    perm_steps = perm.shape[0] // bsz
    pm = bsz * perm_steps
    u = jnp.concatenate(
        [jnp.dot(perm, u_ref[:, t0:t0 + perm_steps, :].reshape(pm, u_ref.shape[2]),
                 preferred_element_type=F32) for t0 in range(0, steps, perm_steps)], axis=0).astype(BF16)

    def project_in(c):
        x_scr[:, 2 * cs * c:2 * cs * (c + 1)] = jnp.dot(
            u[:, CHUNK_IN * c:CHUNK_IN * (c + 1)], wb_ref[c], preferred_element_type=F32)

    def scan(c):
        re0 = 2 * cs * c
        im0 = re0 + cs
        a_re = jnp.broadcast_to(are_ref[:, cs * c:cs * (c + 1)], (bsz, cs))
        a_im = jnp.broadcast_to(aim_ref[:, cs * c:cs * (c + 1)], (bsz, cs))

        def step(t, carry):
            xr, xi = carry
            r0 = pl.multiple_of(t * bsz, bsz)
            nr = a_re * xr - a_im * xi + x_scr[pl.ds(r0, bsz), re0:re0 + cs]
            ni = a_re * xi + a_im * xr + x_scr[pl.ds(r0, bsz), im0:im0 + cs]
            x_scr[pl.ds(r0, bsz), re0:re0 + cs] = nr
            x_scr[pl.ds(r0, bsz), im0:im0 + cs] = ni
            return nr, ni

        xr, xi = lax.fori_loop(0, steps, step,
                               (st_scr[:, re0:re0 + cs], st_scr[:, im0:im0 + cs]), unroll=True)
        st_scr[:, re0:re0 + cs] = xr
        st_scr[:, im0:im0 + cs] = xi

    def project_out(c):
        return jnp.dot(x_scr[:, 2 * cs * c:2 * cs * (c + 1)].astype(BF16), wc_ref[c],
                       preferred_element_type=F32)

    ys = [None] * n_chunks
    project_in(0)
    for c in range(n_chunks):
        if c + 1 < n_chunks:
            project_in(c + 1)
        if c >= 1:
            ys[c - 1] = project_out(c - 1)
        scan(c)
    ys[n_chunks - 1] = project_out(n_chunks - 1)
    y = jnp.concatenate(ys, axis=1) + d_ref[...] * u.astype(F32)
    y = jax.nn.gelu(y)
    z = jnp.dot(y.astype(BF16), wglu_ref[...], preferred_element_type=F32) + bglu_ref[...]
    out = (y * _sigmoid(z)).astype(BF16)
    for i, t0 in enumerate(range(0, steps, perm_steps)):
        part = lax.dot_general(perm, out[i * pm:(i + 1) * pm, :], (((0,), (0,)), ((), ())),
                               preferred_element_type=F32)
        o_ref[:, t0:t0 + perm_steps, :] = (part.astype(BF16).reshape(bsz, perm_steps, part.shape[1])
                                           * zs_ref[:, t0:t0 + perm_steps, :])


def _s5(u_bt, zs_bt, lbar_re, lbar_im, wb, wc, d_row, w_glu, b_glu, bsz, seq):
    ssm_w = u_bt.shape[2]
    n_chunks = wb.shape[0]
    steps = S5_STEPS
    m = steps * bsz
    n_state = n_chunks * CHUNK_STATE
    kern = functools.partial(_s5_kernel, bsz=bsz, steps=steps, n_chunks=n_chunks)
    const2 = lambda i: (0, 0)
    const3 = lambda i: (0, 0, 0)
    rows = pl.BlockSpec((bsz, steps, ssm_w), lambda i: (0, i, 0))
    pm = bsz * S5_PERM_STEPS
    r = jnp.arange(pm)
    perm = (r[None, :] == ((r % bsz) * S5_PERM_STEPS + r // bsz)[:, None]).astype(BF16)
    return pl.pallas_call(
        kern,
        grid=(seq // steps,),
        in_specs=[
            rows,
            rows,
            pl.BlockSpec((pm, pm), const2),
            pl.BlockSpec((1, n_state), const2),
            pl.BlockSpec((1, n_state), const2),
            pl.BlockSpec(wb.shape, const3),
            pl.BlockSpec(wc.shape, const3),
            pl.BlockSpec((1, ssm_w), const2),
            pl.BlockSpec(w_glu.shape, const2),
            pl.BlockSpec((1, ssm_w), const2),
        ],
        out_specs=rows,
        out_shape=jax.ShapeDtypeStruct((bsz, seq, ssm_w), BF16),
        scratch_shapes=[
            pltpu.VMEM((m, 2 * n_state), F32),
            pltpu.VMEM((bsz, 2 * n_state), F32),
        ],
        compiler_params=pltpu.CompilerParams(
            dimension_semantics=("arbitrary",), vmem_limit_bytes=VMEM_LIMIT),
        name="s5_scan",
    )(u_bt, zs_bt, perm, lbar_re, lbar_im, wb, wc, d_row, w_glu, b_glu)


def _s5_block_diag_weights(bbar_re, bbar_im, c_re, c_im):
    hg, n = bbar_re.shape
    g = n // SSM_STATE
    nc = g // GROUPS_PER_CHUNK
    eye = jnp.eye(GROUPS_PER_CHUNK, dtype=F32)

    def wb_part(bb):
        t = bb.reshape(hg, nc, GROUPS_PER_CHUNK, SSM_STATE)
        w = jnp.einsum('ab,hcbp->cahbp', eye, t)
        return w.reshape(nc, GROUPS_PER_CHUNK * hg, GROUPS_PER_CHUNK * SSM_STATE)

    def wc_part(cc):
        t = cc.reshape(nc, GROUPS_PER_CHUNK, hg, SSM_STATE)
        w = jnp.einsum('ab,cbhp->capbh', eye, t)
        return w.reshape(nc, GROUPS_PER_CHUNK * SSM_STATE, GROUPS_PER_CHUNK * hg)

    wb = jnp.concatenate([wb_part(bbar_re), wb_part(bbar_im)], axis=2).astype(BF16)
    wc = jnp.concatenate([wc_part(c_re), -wc_part(c_im)], axis=1).astype(BF16)
    return wb, wc


def _out_kernel(att_ref, ssm_ref, gate_ref, x_ref, woa_ref, wos_ref, wout_ref, fg_ref, o_ref, *, d_model):
    att = jnp.concatenate([att_ref[i] for i in range(att_ref.shape[0])], axis=1)
    ya = jnp.dot(att, woa_ref[...], preferred_element_type=F32)
    ys = jnp.dot(ssm_ref[...], wos_ref[...], preferred_element_type=F32)
    ga = gate_ref[:, 0:d_model].astype(F32)
    gs = gate_ref[:, d_model:2 * d_model].astype(F32)
    merged = ga * ya + gs * ys
    o = x_ref[...] + jnp.dot(merged.astype(BF16), wout_ref[...], preferred_element_type=F32)
    ms = jnp.mean(o * o, axis=-1, keepdims=True)
    o_ref[...] = o * lax.rsqrt(ms + EPS) * fg_ref[...]


def _out_proj(att, ssm, gate, x2, w_o_att, w_o_ssm, w_out, final_g, bsz, seq):
    d_model = x2.shape[1]
    att_w = w_o_att.shape[0]
    ssm_w = w_o_ssm.shape[0]
    tm = ROW_TILE
    nl = seq // tm
    row = lambda b, l: (b * nl + l, 0)
    const = lambda b, l: (0, 0)
    kern = functools.partial(_out_kernel, d_model=d_model)
    return pl.pallas_call(
        kern,
        grid=(bsz, nl),
        in_specs=[
            pl.BlockSpec((None, ATT_HEADS, tm, ATT_V_DIM), lambda b, l: (b, 0, l, 0)),
            pl.BlockSpec((tm, ssm_w), row),
            pl.BlockSpec((tm, 2 * d_model), row),
            pl.BlockSpec((tm, d_model), row),
            pl.BlockSpec(w_o_att.shape, const),
            pl.BlockSpec(w_o_ssm.shape, const),
            pl.BlockSpec(w_out.shape, const),
            pl.BlockSpec((1, d_model), const),
        ],
        out_specs=pl.BlockSpec((tm, d_model), row),
        out_shape=jax.ShapeDtypeStruct((bsz * seq, d_model), F32),
        compiler_params=pltpu.CompilerParams(
            dimension_semantics=("arbitrary", "arbitrary"), vmem_limit_bytes=VMEM_LIMIT),
        name="out_proj",
    )(att, ssm, gate, x2, w_o_att, w_o_ssm, w_out, final_g)


def kernel(x, norm_g, w_in, lambda_q1, lambda_k1, lambda_q2, lambda_k2, subln_g, w_o_att,
           ssm_lambda_re, ssm_lambda_im, ssm_log_dt, ssm_b_re, ssm_b_im, ssm_c_re, ssm_c_im,
           ssm_d, w_glu, b_glu, w_o_ssm, w_out, final_g):
    bsz, seq, d_model = x.shape
    depth = w_in.shape[0]
    assert depth == 1, "single-layer block"
    att_w = w_o_att.shape[1]
    ssm_w = w_o_ssm.shape[1]
    assert att_w == ATT_HEADS * ATT_V_DIM
    assert w_in.shape[2] == 4 * att_w + 2 * ssm_w + 2 * d_model
    assert seq % ROW_TILE == 0 and seq % ATT_BLOCK == 0 and seq % S5_STEPS == 0
    assert S5_STEPS % S5_PERM_STEPS == 0
    assert bsz % 16 == 0, "batch rows fill whole bf16 sublane tiles in the S5 kernel"
    assert (ssm_w // SSM_GROUP) % GROUPS_PER_CHUNK == 0

    x2 = x.reshape(bsz * seq, d_model)
    qkv, za, u, zs, gate = _in_proj(x2, norm_g[0][None], w_in[0].astype(BF16), bsz, seq, att_w, ssm_w)

    slopes = jnp.asarray([2.0 ** (-8.0 * (h + 1) / ATT_HEADS) for h in range(ATT_HEADS)], F32)
    att = _attention(qkv, za, slopes, lambda_q1, lambda_k1, lambda_q2, lambda_k2, subln_g, bsz, seq)

    lbar_re, lbar_im, bbar_re, bbar_im = _s5_discretise(
        ssm_lambda_re[0], ssm_lambda_im[0], ssm_log_dt[0], ssm_b_re[0], ssm_b_im[0])
    wb, wc = _s5_block_diag_weights(bbar_re, bbar_im, ssm_c_re[0], ssm_c_im[0])
    ssm = _s5(u.reshape(bsz, seq, ssm_w), zs.reshape(bsz, seq, ssm_w), lbar_re, lbar_im, wb, wc,
              ssm_d[0].reshape(1, ssm_w), w_glu[0].astype(BF16), b_glu[0][None], bsz, seq)

    out = _out_proj(att, ssm.reshape(bsz * seq, ssm_w), gate, x2,
                    w_o_att[0].astype(BF16), w_o_ssm[0].astype(BF16), w_out[0].astype(BF16),
                    final_g[None], bsz, seq)
    return out.reshape(bsz, seq, d_model)
```

```python
import functools
import math

import jax
import jax.numpy as jnp
from jax import lax
from jax.experimental import pallas as pl
from jax.experimental.pallas import tpu as pltpu

EPS = 1e-5
ATT_HEADS = 8
ATT_HEAD_DIM = 64
ATT_V_DIM = 2 * ATT_HEAD_DIM
SSM_GROUP = 16
SSM_STATE = 64
LAMBDA_INIT = 0.8 - 0.6 * math.exp(-0.3 * 0)
LOG2E = math.log2(math.e)
Q_PRESCALE = ATT_HEAD_DIM ** -0.5 * LOG2E
BIAS_TERMS = 3

GROUPS_PER_CHUNK = 8
CHUNK_IN = GROUPS_PER_CHUNK * SSM_GROUP
CHUNK_STATE = GROUPS_PER_CHUNK * SSM_STATE

ROW_TILE = 1024
ATT_BLOCK = 128
ATT_SUM_ROWS = 16
ATT_VALUE_LAG = 7
S5_STEPS = 64
S5_PERM_STEPS = 32
VMEM_LIMIT = 56 * 1024 * 1024

F32 = jnp.float32
BF16 = jnp.bfloat16


def _sigmoid(x):
    return 0.5 * jnp.tanh(0.5 * x) + 0.5


def _silu(x):
    return x * _sigmoid(x)


def _in_proj_kernel(x_ref, g_ref, w_ref, qkv_ref, za_ref, u_ref, zs_ref, gate_ref, *, att_w, ssm_w, d_model):
    x = x_ref[...]
    ms = jnp.mean(x * x, axis=-1, keepdims=True)
    h = (x * lax.rsqrt(ms + EPS) * g_ref[...]).astype(BF16)

    def proj(dst_ref, dst0, src0, width, fn=None, step=512):
        for o in range(0, width, step):
            r = jnp.dot(h, w_ref[:, src0 + o:src0 + o + step], preferred_element_type=F32)
            if fn is not None:
                r = fn(r)
            dst_ref[:, dst0 + o:dst0 + o + step] = r.astype(dst_ref.dtype)

    q0, z0 = 0, 3 * att_w
    u0 = z0 + att_w
    r0 = u0 + ssm_w
    proj(za_ref, 0, z0, att_w, fn=_silu)
    proj(zs_ref, 0, r0, ssm_w, fn=_silu)
    proj(gate_ref, 0, r0 + ssm_w, 2 * d_model, fn=_sigmoid)
    proj(qkv_ref, 0, q0, att_w, fn=lambda r: r * Q_PRESCALE)
    proj(u_ref, 0, u0, ssm_w)
    proj(qkv_ref, att_w, q0 + att_w, 2 * att_w)


def _in_proj(x2, norm_g, w_in, bsz, seq, att_w, ssm_w):
    d_model = x2.shape[1]
    tm = ROW_TILE
    nl = seq // tm
    kern = functools.partial(_in_proj_kernel, att_w=att_w, ssm_w=ssm_w, d_model=d_model)
    row = lambda b, l: (b * nl + l, 0)
    return pl.pallas_call(
        kern,
        grid=(bsz, nl),
        in_specs=[
            pl.BlockSpec((tm, d_model), row),
            pl.BlockSpec((1, d_model), lambda b, l: (0, 0)),
            pl.BlockSpec(w_in.shape, lambda b, l: (0, 0), pipeline_mode=pl.Buffered(1)),
        ],
        out_specs=[
            pl.BlockSpec((tm, 3 * att_w), row),
            pl.BlockSpec((tm, att_w), row),
            pl.BlockSpec((tm, ssm_w), row),
            pl.BlockSpec((tm, ssm_w), row),
            pl.BlockSpec((tm, 2 * d_model), row),
        ],
        out_shape=[
            jax.ShapeDtypeStruct((bsz * seq, 3 * att_w), BF16),
            jax.ShapeDtypeStruct((bsz * seq, att_w), BF16),
            jax.ShapeDtypeStruct((bsz * seq, ssm_w), BF16),
            jax.ShapeDtypeStruct((bsz * seq, ssm_w), BF16),
            jax.ShapeDtypeStruct((bsz * seq, 2 * d_model), BF16),
        ],
        compiler_params=pltpu.CompilerParams(
            dimension_semantics=("arbitrary", "arbitrary"), vmem_limit_bytes=VMEM_LIMIT),
        name="in_proj",
    )(x2, norm_g, w_in)


def _split_bf16(v, n):
    parts = []
    for _ in range(n):
        p = v.astype(BF16)
        parts.append(p)
        v = v - p.astype(F32)
    return parts


def _attn_kernel(slopes_ref, zero_ref, lq1_ref, lk1_ref, lq2_ref, lk2_ref, sg_ref, q_ref, k_ref, v_ref,
                 za_ref, o_ref, kx_ref, vt_ref, *sp_refs):
    blk = ATT_BLOCK
    hd = ATT_HEAD_DIM
    vd = ATT_V_DIM
    n_blk = k_ref.shape[0] // blk
    s_refs, p_refs = sp_refs[:n_blk], sp_refs[n_blk:]

    slope2 = slopes_ref[pl.program_id(1)] * LOG2E
    lane = lax.broadcasted_iota(jnp.int32, (blk, vd), 1)
    pos = lax.broadcasted_iota(jnp.int32, (blk, vd), 0)
    local = jnp.zeros((blk, vd), F32)
    for t, p in enumerate(_split_bf16(slope2 * pos.astype(F32), BIAS_TERMS)):
        local = jnp.where(lane == t, p.astype(F32), local)
    lane_row = lax.broadcasted_iota(jnp.int32, (1, vd), 1)
    for j in range(n_blk):
        rows = pl.ds(j * blk, blk)
        offset = jnp.zeros((1, vd), F32)
        for t, p in enumerate(_split_bf16(slope2 * jnp.full((1, vd), j * blk, F32), BIAS_TERMS)):
            offset = jnp.where(lane_row == BIAS_TERMS + t, p.astype(F32), offset)
        kx_ref[rows, 0:vd] = k_ref[rows, :]
        kx_ref[rows, vd:2 * vd] = (local + offset).astype(BF16)
        vt_ref[0:vd, j * blk:(j + 1) * blk] = v_ref[rows, :].astype(F32).T.astype(BF16)
    sub = lax.broadcasted_iota(jnp.int32, (ATT_SUM_ROWS, vt_ref.shape[1]), 0)
    vt_ref[vd:vd + ATT_SUM_ROWS, :] = jnp.where(sub == 0, 1.0, 0.0).astype(BF16)

    e1 = jnp.exp(jnp.sum(lq1_ref[...] * lk1_ref[...], axis=-1, keepdims=True))
    e2 = jnp.exp(jnp.sum(lq2_ref[...] * lk2_ref[...], axis=-1, keepdims=True))
    lam = e1 - e2 + LAMBDA_INIT
    gain = sg_ref[...] * (1.0 - LAMBDA_INIT)
    nt = (((1,), (1,)), ((), ()))

    key = lax.broadcasted_iota(jnp.int32, (blk, 2 * blk), 0)
    qry = lax.broadcasted_iota(jnp.int32, (blk, 2 * blk), 1)
    causal = key <= jnp.where(qry >= blk, qry - blk, qry)

    dyn0 = zero_ref[0]
    col_max = {}

    def score_stage(c):
        nk = (c + 1) * blk
        q = q_ref[c * blk:(c + 1) * blk, :].astype(F32)
        zero = jnp.zeros_like(q)
        ones = jnp.where(lane < 2 * BIAS_TERMS, jnp.ones_like(q), zero)
        qx = jnp.concatenate([
            jnp.concatenate([jnp.where(lane < hd, q, zero), ones], axis=1),
            jnp.concatenate([jnp.where(lane >= hd, q, zero), ones], axis=1)], axis=0).astype(BF16)
        s = lax.dot_general(kx_ref[0:nk, :], qx, nt, preferred_element_type=F32)
        m = None
        for j in range(c + 1):
            sj = s[j * blk:(j + 1) * blk, :]
            if j == c:
                sj = jnp.where(causal, sj, -jnp.inf)
            s_refs[c][j * blk:(j + 1) * blk, :] = sj
            mj = jnp.max(sj, axis=0, keepdims=True)
            m = mj if m is None else jnp.maximum(m, mj)
        col_max[c] = m

    def softmax_stage(c):
        for j in range(c + 1):
            sj = s_refs[c][pl.ds(pl.multiple_of(dyn0 + j * blk, blk), blk), :]
            p_refs[c][j * blk:(j + 1) * blk, :] = jnp.exp2(sj - col_max[c]).astype(BF16)

    def value_stage(c):
        nk = (c + 1) * blk
        p = p_refs[c][pl.ds(pl.multiple_of(dyn0, blk), nk), :]
        acc = jnp.dot(vt_ref[:, 0:nk], p, preferred_element_type=F32)
        r = 1.0 / acc[vd:vd + 1, :]
        ot = acc[0:vd, 0:blk] * r[:, 0:blk] - acc[0:vd, blk:2 * blk] * (lam * r[:, blk:2 * blk])
        ms = jnp.mean(ot * ot, axis=0, keepdims=True)
        att = (ot * lax.rsqrt(ms + EPS)).T * gain
        o_ref[c * blk:(c + 1) * blk, :] = att.astype(BF16) * za_ref[c * blk:(c + 1) * blk, :]

    order = list(reversed(range(n_blk)))
    lag = lambda i: ATT_VALUE_LAG if i < 4 else ATT_VALUE_LAG - 2
    emitted = 0
    for t in range(n_blk):
        score_stage(order[t])
        softmax_stage(order[t])
        while emitted < n_blk and emitted + lag(emitted) <= t:
            value_stage(order[emitted])
            emitted += 1
    while emitted < n_blk:
        value_stage(order[emitted])
        emitted += 1


def _attention(qkv, za, slopes, lq1, lk1, lq2, lk2, subln_g, bsz, seq):
    blk = ATT_BLOCK
    n_blk = seq // blk
    nh = ATT_HEADS
    vec = pl.BlockSpec((1, ATT_HEAD_DIM), lambda b, h: (0, 0))
    head = lambda first: pl.BlockSpec((seq, ATT_V_DIM), lambda b, h: (b, first + h))
    return pl.pallas_call(
        _attn_kernel,
        grid=(bsz, nh),
        in_specs=[
            pl.BlockSpec(memory_space=pltpu.SMEM),
            pl.BlockSpec(memory_space=pltpu.SMEM),
            vec, vec, vec, vec,
            pl.BlockSpec((1, ATT_V_DIM), lambda b, h: (0, 0)),
            head(0), head(nh), head(2 * nh),
            head(0),
        ],
        out_specs=head(0),
        out_shape=jax.ShapeDtypeStruct((bsz * seq, nh * ATT_V_DIM), BF16),
        scratch_shapes=[
            pltpu.VMEM((seq, 2 * ATT_V_DIM), BF16),
            pltpu.VMEM((ATT_V_DIM + ATT_SUM_ROWS, seq), BF16),
        ] + [pltpu.VMEM(((c + 1) * blk, 2 * blk), F32) for c in range(n_blk)]
          + [pltpu.VMEM(((c + 1) * blk, 2 * blk), BF16) for c in range(n_blk)],
        compiler_params=pltpu.CompilerParams(
            dimension_semantics=("arbitrary", "arbitrary"), vmem_limit_bytes=VMEM_LIMIT),
        name="diff_attention",
    )(slopes, jnp.zeros((1,), jnp.int32), lq1, lk1, lq2, lk2, subln_g, qkv, qkv, qkv, za)


def _s5_disc_kernel(lre_ref, lim_ref, ldt_ref, bre_ref, bim_ref, are_ref, aim_ref, bbre_ref, bbim_ref):
    dt = jnp.exp(ldt_ref[...])
    lre = jnp.minimum(lre_ref[...], -1e-4)
    lim = lim_ref[...]
    mag = jnp.exp(lre * dt)
    lbar_re = mag * jnp.cos(lim * dt)
    lbar_im = mag * jnp.sin(lim * dt)
    num_re = lbar_re - 1.0
    den = lre * lre + lim * lim
    coef_re = (num_re * lre + lbar_im * lim) / den
    coef_im = (lbar_im * lre - num_re * lim) / den
    bre = bre_ref[...]
    bim = bim_ref[...]
    are_ref[...] = lbar_re
    aim_ref[...] = lbar_im
    bbre_ref[...] = coef_re * bre - coef_im * bim
    bbim_ref[...] = coef_re * bim + coef_im * bre


def _s5_discretise(lam_re, lam_im, log_dt, b_re, b_im):
    g, p = lam_re.shape
    hg = b_re.shape[-1]
    n = g * p
    row = lambda a: a.reshape(1, n)
    tr = lambda a: a.reshape(n, hg).T
    ldt = jnp.broadcast_to(log_dt[:, None], (g, p))
    return pl.pallas_call(
        _s5_disc_kernel,
        out_shape=[
            jax.ShapeDtypeStruct((1, n), F32), jax.ShapeDtypeStruct((1, n), F32),
            jax.ShapeDtypeStruct((hg, n), F32), jax.ShapeDtypeStruct((hg, n), F32),
        ],
        name="s5_discretise",
    )(row(lam_re), row(lam_im), row(ldt), tr(b_re), tr(b_im))


def _s5_kernel(u_ref, zs_ref, perm_ref, are_ref, aim_ref, wb_ref, wc_ref, d_ref, wglu_ref, bglu_ref, o_ref,
               x_scr, st_scr, *, bsz, steps, n_chunks):
    @pl.when(pl.program_id(0) == 0)
    def _():
        st_scr[...] = jnp.zeros_like(st_scr)

    cs = CHUNK_STATE
    perm = perm_ref[...]
    perm_steps = perm.shape[0] // bsz
    pm = bsz * perm_steps
    u = jnp.concatenate(
        [jnp.dot(perm, u_ref[:, t0:t0 + perm_steps, :].reshape(pm, u_ref.shape[2]),
                 preferred_element_type=F32) for t0 in range(0, steps, perm_steps)], axis=0).astype(BF16)

    def project_in(c):
        x_scr[:, 2 * cs * c:2 * cs * (c + 1)] = jnp.dot(
            u[:, CHUNK_IN * c:CHUNK_IN * (c + 1)], wb_ref[c], preferred_element_type=F32)

    def scan(c):
        re0 = 2 * cs * c
        im0 = re0 + cs
        a_re = jnp.broadcast_to(are_ref[:, cs * c:cs * (c + 1)], (bsz, cs))
        a_im = jnp.broadcast_to(aim_ref[:, cs * c:cs * (c + 1)], (bsz, cs))

        def step(t, carry):
            xr, xi = carry
            r0 = pl.multiple_of(t * bsz, bsz)
            nr = a_re * xr - a_im * xi + x_scr[pl.ds(r0, bsz), re0:re0 + cs]
            ni = a_re * xi + a_im * xr + x_scr[pl.ds(r0, bsz), im0:im0 + cs]
            x_scr[pl.ds(r0, bsz), re0:re0 + cs] = nr
            x_scr[pl.ds(r0, bsz), im0:im0 + cs] = ni
            return nr, ni

        xr, xi = lax.fori_loop(0, steps, step,
                               (st_scr[:, re0:re0 + cs], st_scr[:, im0:im0 + cs]), unroll=True)
        st_scr[:, re0:re0 + cs] = xr
        st_scr[:, im0:im0 + cs] = xi

    def project_out(c):
        return jnp.dot(x_scr[:, 2 * cs * c:2 * cs * (c + 1)].astype(BF16), wc_ref[c],
                       preferred_element_type=F32)

    ys = [None] * n_chunks
    project_in(0)
    for c in range(n_chunks):
        if c + 1 < n_chunks:
            project_in(c + 1)
        if c >= 1:
            ys[c - 1] = project_out(c - 1)
        scan(c)
    ys[n_chunks - 1] = project_out(n_chunks - 1)
    y = jnp.concatenate(ys, axis=1) + d_ref[...] * u.astype(F32)
    y = jax.nn.gelu(y)
    z = jnp.dot(y.astype(BF16), wglu_ref[...], preferred_element_type=F32) + bglu_ref[...]
    out = (y * _sigmoid(z)).astype(BF16)
    for i, t0 in enumerate(range(0, steps, perm_steps)):
        part = lax.dot_general(perm, out[i * pm:(i + 1) * pm, :], (((0,), (0,)), ((), ())),
                               preferred_element_type=F32)
        o_ref[:, t0:t0 + perm_steps, :] = (part.astype(BF16).reshape(bsz, perm_steps, part.shape[1])
                                           * zs_ref[:, t0:t0 + perm_steps, :])


def _s5(u_bt, zs_bt, lbar_re, lbar_im, wb, wc, d_row, w_glu, b_glu, bsz, seq):
    ssm_w = u_bt.shape[2]
    n_chunks = wb.shape[0]
    steps = S5_STEPS
    m = steps * bsz
    n_state = n_chunks * CHUNK_STATE
    kern = functools.partial(_s5_kernel, bsz=bsz, steps=steps, n_chunks=n_chunks)
    const2 = lambda i: (0, 0)
    const3 = lambda i: (0, 0, 0)
    rows = pl.BlockSpec((bsz, steps, ssm_w), lambda i: (0, i, 0))
    pm = bsz * S5_PERM_STEPS
    r = jnp.arange(pm)
    perm = (r[None, :] == ((r % bsz) * S5_PERM_STEPS + r // bsz)[:, None]).astype(BF16)
    return pl.pallas_call(
        kern,
        grid=(seq // steps,),
        in_specs=[
            rows,
            rows,
            pl.BlockSpec((pm, pm), const2),
            pl.BlockSpec((1, n_state), const2),
            pl.BlockSpec((1, n_state), const2),
            pl.BlockSpec(wb.shape, const3),
            pl.BlockSpec(wc.shape, const3),
            pl.BlockSpec((1, ssm_w), const2),
            pl.BlockSpec(w_glu.shape, const2),
            pl.BlockSpec((1, ssm_w), const2),
        ],
        out_specs=rows,
        out_shape=jax.ShapeDtypeStruct((bsz, seq, ssm_w), BF16),
        scratch_shapes=[
            pltpu.VMEM((m, 2 * n_state), F32),
            pltpu.VMEM((bsz, 2 * n_state), F32),
        ],
        compiler_params=pltpu.CompilerParams(
            dimension_semantics=("arbitrary",), vmem_limit_bytes=VMEM_LIMIT),
        name="s5_scan",
    )(u_bt, zs_bt, perm, lbar_re, lbar_im, wb, wc, d_row, w_glu, b_glu)


def _s5_block_diag_weights(bbar_re, bbar_im, c_re, c_im):
    hg, n = bbar_re.shape
    g = n // SSM_STATE
    nc = g // GROUPS_PER_CHUNK
    eye = jnp.eye(GROUPS_PER_CHUNK, dtype=F32)

    def wb_part(bb):
        t = bb.reshape(hg, nc, GROUPS_PER_CHUNK, SSM_STATE)
        w = jnp.einsum('ab,hcbp->cahbp', eye, t)
        return w.reshape(nc, GROUPS_PER_CHUNK * hg, GROUPS_PER_CHUNK * SSM_STATE)

    def wc_part(cc):
        t = cc.reshape(nc, GROUPS_PER_CHUNK, hg, SSM_STATE)
        w = jnp.einsum('ab,cbhp->capbh', eye, t)
        return w.reshape(nc, GROUPS_PER_CHUNK * SSM_STATE, GROUPS_PER_CHUNK * hg)

    wb = jnp.concatenate([wb_part(bbar_re), wb_part(bbar_im)], axis=2).astype(BF16)
    wc = jnp.concatenate([wc_part(c_re), -wc_part(c_im)], axis=1).astype(BF16)
    return wb, wc


def _out_kernel(att_ref, ssm_ref, gate_ref, x_ref, woa_ref, wos_ref, wout_ref, fg_ref, o_ref, *, d_model):
    ya = jnp.dot(att_ref[...], woa_ref[...], preferred_element_type=F32)
    ys = jnp.dot(ssm_ref[...], wos_ref[...], preferred_element_type=F32)
    ga = gate_ref[:, 0:d_model].astype(F32)
    gs = gate_ref[:, d_model:2 * d_model].astype(F32)
    merged = ga * ya + gs * ys
    o = x_ref[...] + jnp.dot(merged.astype(BF16), wout_ref[...], preferred_element_type=F32)
    ms = jnp.mean(o * o, axis=-1, keepdims=True)
    o_ref[...] = o * lax.rsqrt(ms + EPS) * fg_ref[...]


def _out_proj(att, ssm, gate, x2, w_o_att, w_o_ssm, w_out, final_g, bsz, seq):
    d_model = x2.shape[1]
    att_w = att.shape[1]
    ssm_w = w_o_ssm.shape[0]
    tm = ROW_TILE
    nl = seq // tm
    row = lambda b, l: (b * nl + l, 0)
    const = lambda b, l: (0, 0)
    kern = functools.partial(_out_kernel, d_model=d_model)
    return pl.pallas_call(
        kern,
        grid=(bsz, nl),
        in_specs=[
            pl.BlockSpec((tm, att_w), row),
            pl.BlockSpec((tm, ssm_w), row),
            pl.BlockSpec((tm, 2 * d_model), row),
            pl.BlockSpec((tm, d_model), row),
            pl.BlockSpec(w_o_att.shape, const),
            pl.BlockSpec(w_o_ssm.shape, const),
            pl.BlockSpec(w_out.shape, const),
            pl.BlockSpec((1, d_model), const),
        ],
        out_specs=pl.BlockSpec((tm, d_model), row),
        out_shape=jax.ShapeDtypeStruct((bsz * seq, d_model), F32),
        compiler_params=pltpu.CompilerParams(
            dimension_semantics=("arbitrary", "arbitrary"), vmem_limit_bytes=VMEM_LIMIT),
        name="out_proj",
    )(att, ssm, gate, x2, w_o_att, w_o_ssm, w_out, final_g)


def kernel(x, norm_g, w_in, lambda_q1, lambda_k1, lambda_q2, lambda_k2, subln_g, w_o_att,
           ssm_lambda_re, ssm_lambda_im, ssm_log_dt, ssm_b_re, ssm_b_im, ssm_c_re, ssm_c_im,
           ssm_d, w_glu, b_glu, w_o_ssm, w_out, final_g):
    bsz, seq, d_model = x.shape
    depth = w_in.shape[0]
    assert depth == 1, "single-layer block"
    att_w = w_o_att.shape[1]
    ssm_w = w_o_ssm.shape[1]
    assert att_w == ATT_HEADS * ATT_V_DIM
    assert w_in.shape[2] == 4 * att_w + 2 * ssm_w + 2 * d_model
    assert seq % ROW_TILE == 0 and seq % ATT_BLOCK == 0 and seq % S5_STEPS == 0
    assert S5_STEPS % S5_PERM_STEPS == 0
    assert bsz % 16 == 0, "batch rows fill whole bf16 sublane tiles in the S5 kernel"
    assert (ssm_w // SSM_GROUP) % GROUPS_PER_CHUNK == 0

    x2 = x.reshape(bsz * seq, d_model)
    qkv, za, u, zs, gate = _in_proj(x2, norm_g[0][None], w_in[0].astype(BF16), bsz, seq, att_w, ssm_w)

    slopes = jnp.asarray([2.0 ** (-8.0 * (h + 1) / ATT_HEADS) for h in range(ATT_HEADS)], F32)
    att = _attention(qkv, za, slopes, lambda_q1, lambda_k1, lambda_q2, lambda_k2, subln_g, bsz, seq)

    lbar_re, lbar_im, bbar_re, bbar_im = _s5_discretise(
        ssm_lambda_re[0], ssm_lambda_im[0], ssm_log_dt[0], ssm_b_re[0], ssm_b_im[0])
    wb, wc = _s5_block_diag_weights(bbar_re, bbar_im, ssm_c_re[0], ssm_c_im[0])
    ssm = _s5(u.reshape(bsz, seq, ssm_w), zs.reshape(bsz, seq, ssm_w), lbar_re, lbar_im, wb, wc,
              ssm_d[0].reshape(1, ssm_w), w_glu[0].astype(BF16), b_glu[0][None], bsz, seq)

    out = _out_proj(att, ssm.reshape(bsz * seq, ssm_w), gate, x2,
                    w_o_att[0].astype(BF16), w_o_ssm[0].astype(BF16), w_out[0].astype(BF16),
                    final_g[None], bsz, seq)
    return out.reshape(bsz, seq, d_model)
```

```python
import functools
import math

import jax
import jax.numpy as jnp
from jax import lax
from jax.experimental import pallas as pl
from jax.experimental.pallas import tpu as pltpu

EPS = 1e-5
ATT_HEADS = 8
ATT_HEAD_DIM = 64
ATT_V_DIM = 2 * ATT_HEAD_DIM
SSM_GROUP = 16
SSM_STATE = 64
LAMBDA_INIT = 0.8 - 0.6 * math.exp(-0.3 * 0)
LOG2E = math.log2(math.e)
Q_PRESCALE = ATT_HEAD_DIM ** -0.5 * LOG2E
BIAS_TERMS = 3

GROUPS_PER_CHUNK = 8
CHUNK_IN = GROUPS_PER_CHUNK * SSM_GROUP
CHUNK_STATE = GROUPS_PER_CHUNK * SSM_STATE

ROW_TILE = 1024
ATT_BLOCK = 128
ATT_SUM_ROWS = 16
ATT_VALUE_LAG = 7
S5_STEPS = 64
S5_PERM_STEPS = 32
VMEM_LIMIT = 56 * 1024 * 1024

F32 = jnp.float32
BF16 = jnp.bfloat16


def _sigmoid(x):
    return 0.5 * jnp.tanh(0.5 * x) + 0.5


def _silu(x):
    return x * _sigmoid(x)


def _in_proj_kernel(x_ref, g_ref, w_ref, qkv_ref, za_ref, u_ref, zs_ref, gate_ref, *, att_w, ssm_w, d_model):
    x = x_ref[...]
    ms = jnp.mean(x * x, axis=-1, keepdims=True)
    h = (x * lax.rsqrt(ms + EPS) * g_ref[...]).astype(BF16)

    def proj(dst_ref, dst0, src0, width, fn=None, step=512):
        for o in range(0, width, step):
            r = jnp.dot(h, w_ref[:, src0 + o:src0 + o + step], preferred_element_type=F32)
            if fn is not None:
                r = fn(r)
            dst_ref[:, dst0 + o:dst0 + o + step] = r.astype(dst_ref.dtype)

    q0, z0 = 0, 3 * att_w
    u0 = z0 + att_w
    r0 = u0 + ssm_w
    proj(za_ref, 0, z0, att_w, fn=_silu)
    proj(zs_ref, 0, r0, ssm_w, fn=_silu)
    proj(gate_ref, 0, r0 + ssm_w, 2 * d_model, fn=_sigmoid)
    proj(qkv_ref, 0, q0, att_w, fn=lambda r: r * Q_PRESCALE)
    proj(u_ref, 0, u0, ssm_w)
    proj(qkv_ref, att_w, q0 + att_w, 2 * att_w)


def _in_proj(x2, norm_g, w_in, bsz, seq, att_w, ssm_w):
    d_model = x2.shape[1]
    tm = ROW_TILE
    nl = seq // tm
    kern = functools.partial(_in_proj_kernel, att_w=att_w, ssm_w=ssm_w, d_model=d_model)
    row = lambda b, l: (b * nl + l, 0)
    return pl.pallas_call(
        kern,
        grid=(bsz, nl),
        in_specs=[
            pl.BlockSpec((tm, d_model), row),
            pl.BlockSpec((1, d_model), lambda b, l: (0, 0)),
            pl.BlockSpec(w_in.shape, lambda b, l: (0, 0), pipeline_mode=pl.Buffered(1)),
        ],
        out_specs=[
            pl.BlockSpec((tm, 3 * att_w), row),
            pl.BlockSpec((tm, att_w), row),
            pl.BlockSpec((tm, ssm_w), row),
            pl.BlockSpec((tm, ssm_w), row),
            pl.BlockSpec((tm, 2 * d_model), row),
        ],
        out_shape=[
            jax.ShapeDtypeStruct((bsz * seq, 3 * att_w), BF16),
            jax.ShapeDtypeStruct((bsz * seq, att_w), BF16),
            jax.ShapeDtypeStruct((bsz * seq, ssm_w), BF16),
            jax.ShapeDtypeStruct((bsz * seq, ssm_w), BF16),
            jax.ShapeDtypeStruct((bsz * seq, 2 * d_model), BF16),
        ],
        compiler_params=pltpu.CompilerParams(
            dimension_semantics=("arbitrary", "arbitrary"), vmem_limit_bytes=VMEM_LIMIT),
        name="in_proj",
    )(x2, norm_g, w_in)


def _split_bf16(v, n):
    parts = []
    for _ in range(n):
        p = v.astype(BF16)
        parts.append(p)
        v = v - p.astype(F32)
    return parts


def _attn_kernel(slopes_ref, zero_ref, lq1_ref, lk1_ref, lq2_ref, lk2_ref, sg_ref, q_ref, k_ref, v_ref,
                 za_ref, o_ref, kx_ref, vt_ref, *sp_refs):
    blk = ATT_BLOCK
    hd = ATT_HEAD_DIM
    vd = ATT_V_DIM
    n_blk = k_ref.shape[0] // blk
    s_refs, p_refs = sp_refs[:n_blk], sp_refs[n_blk:]

    @pl.when(zero_ref[0] == 0)
    def _():
        slope2 = slopes_ref[pl.program_id(1)] * LOG2E
        lane = lax.broadcasted_iota(jnp.int32, (blk, vd), 1)
        pos = lax.broadcasted_iota(jnp.int32, (blk, vd), 0)
        local = jnp.zeros((blk, vd), F32)
        for t, p in enumerate(_split_bf16(slope2 * pos.astype(F32), BIAS_TERMS)):
            local = jnp.where(lane == t, p.astype(F32), local)
        lane_row = lax.broadcasted_iota(jnp.int32, (1, vd), 1)
        for j in range(n_blk):
            rows = pl.ds(j * blk, blk)
            offset = jnp.zeros((1, vd), F32)
            for t, p in enumerate(_split_bf16(slope2 * jnp.full((1, vd), j * blk, F32), BIAS_TERMS)):
                offset = jnp.where(lane_row == BIAS_TERMS + t, p.astype(F32), offset)
            kx_ref[rows, 0:vd] = k_ref[rows, :]
            kx_ref[rows, vd:2 * vd] = (local + offset).astype(BF16)
            vt_ref[0:vd, j * blk:(j + 1) * blk] = v_ref[rows, :].astype(F32).T.astype(BF16)
        sub = lax.broadcasted_iota(jnp.int32, (ATT_SUM_ROWS, vt_ref.shape[1]), 0)
        vt_ref[vd:vd + ATT_SUM_ROWS, :] = jnp.where(sub == 0, 1.0, 0.0).astype(BF16)

    e1 = jnp.exp(jnp.sum(lq1_ref[...] * lk1_ref[...], axis=-1, keepdims=True))
    e2 = jnp.exp(jnp.sum(lq2_ref[...] * lk2_ref[...], axis=-1, keepdims=True))
    lam = e1 - e2 + LAMBDA_INIT
    gain = sg_ref[...] * (1.0 - LAMBDA_INIT)
    nt = (((1,), (1,)), ((), ()))

    key = lax.broadcasted_iota(jnp.int32, (blk, 2 * blk), 0)
    qry = lax.broadcasted_iota(jnp.int32, (blk, 2 * blk), 1)
    causal = key <= jnp.where(qry >= blk, qry - blk, qry)
    lane = lax.broadcasted_iota(jnp.int32, (blk, vd), 1)

    dyn0 = zero_ref[0]
    col_max = {}

    def score_stage(c):
        nk = (c + 1) * blk
        q = q_ref[c * blk:(c + 1) * blk, :].astype(F32)
        zero = jnp.zeros_like(q)
        ones = jnp.where(lane < 2 * BIAS_TERMS, jnp.ones_like(q), zero)
        qx = jnp.concatenate([
            jnp.concatenate([jnp.where(lane < hd, q, zero), ones], axis=1),
            jnp.concatenate([jnp.where(lane >= hd, q, zero), ones], axis=1)], axis=0).astype(BF16)
        s = lax.dot_general(kx_ref[0:nk, :], qx, nt, preferred_element_type=F32)
        m = None
        for j in range(c + 1):
            sj = s[j * blk:(j + 1) * blk, :]
            if j == c:
                sj = jnp.where(causal, sj, -jnp.inf)
            s_refs[c][j * blk:(j + 1) * blk, :] = sj
            mj = jnp.max(sj, axis=0, keepdims=True)
            m = mj if m is None else jnp.maximum(m, mj)
        col_max[c] = m

    def softmax_stage(c):
        for j in range(c + 1):
            sj = s_refs[c][pl.ds(pl.multiple_of(dyn0 + j * blk, blk), blk), :]
            p_refs[c][j * blk:(j + 1) * blk, :] = jnp.exp2(sj - col_max[c]).astype(BF16)

    def value_stage(c):
        nk = (c + 1) * blk
        p = p_refs[c][pl.ds(pl.multiple_of(dyn0, blk), nk), :]
        acc = jnp.dot(vt_ref[:, 0:nk], p, preferred_element_type=F32)
        r = 1.0 / acc[vd:vd + 1, :]
        ot = acc[0:vd, 0:blk] * r[:, 0:blk] - acc[0:vd, blk:2 * blk] * (lam * r[:, blk:2 * blk])
        ms = jnp.mean(ot * ot, axis=0, keepdims=True)
        att = (ot * lax.rsqrt(ms + EPS)).T * gain
        o_ref[c * blk:(c + 1) * blk, :] = att.astype(BF16) * za_ref[c * blk:(c + 1) * blk, :]

    order = list(reversed(range(n_blk)))
    for t in range(n_blk + ATT_VALUE_LAG):
        if t < n_blk:
            score_stage(order[t])
            softmax_stage(order[t])
        if t >= ATT_VALUE_LAG:
            value_stage(order[t - ATT_VALUE_LAG])


def _attention(qkv, za, slopes, lq1, lk1, lq2, lk2, subln_g, bsz, seq):
    blk = ATT_BLOCK
    n_blk = seq // blk
    nh = ATT_HEADS
    vec = pl.BlockSpec((1, ATT_HEAD_DIM), lambda b, h: (0, 0))
    head = lambda first: pl.BlockSpec((seq, ATT_V_DIM), lambda b, h: (b, first + h))
    return pl.pallas_call(
        _attn_kernel,
        grid=(bsz, nh),
        in_specs=[
            pl.BlockSpec(memory_space=pltpu.SMEM),
            pl.BlockSpec(memory_space=pltpu.SMEM),
            vec, vec, vec, vec,
            pl.BlockSpec((1, ATT_V_DIM), lambda b, h: (0, 0)),
            head(0), head(nh), head(2 * nh),
            head(0),
        ],
        out_specs=head(0),
        out_shape=jax.ShapeDtypeStruct((bsz * seq, nh * ATT_V_DIM), BF16),
        scratch_shapes=[
            pltpu.VMEM((seq, 2 * ATT_V_DIM), BF16),
            pltpu.VMEM((ATT_V_DIM + ATT_SUM_ROWS, seq), BF16),
        ] + [pltpu.VMEM(((c + 1) * blk, 2 * blk), F32) for c in range(n_blk)]
          + [pltpu.VMEM(((c + 1) * blk, 2 * blk), BF16) for c in range(n_blk)],
        compiler_params=pltpu.CompilerParams(
            dimension_semantics=("arbitrary", "arbitrary"), vmem_limit_bytes=VMEM_LIMIT),
        name="diff_attention",
    )(slopes, jnp.zeros((1,), jnp.int32), lq1, lk1, lq2, lk2, subln_g, qkv, qkv, qkv, za)


def _s5_disc_kernel(lre_ref, lim_ref, ldt_ref, bre_ref, bim_ref, are_ref, aim_ref, bbre_ref, bbim_ref):
    dt = jnp.exp(ldt_ref[...])
    lre = jnp.minimum(lre_ref[...], -1e-4)
    lim = lim_ref[...]
    mag = jnp.exp(lre * dt)
    lbar_re = mag * jnp.cos(lim * dt)
    lbar_im = mag * jnp.sin(lim * dt)
    num_re = lbar_re - 1.0
    den = lre * lre + lim * lim
    coef_re = (num_re * lre + lbar_im * lim) / den
    coef_im = (lbar_im * lre - num_re * lim) / den
    bre = bre_ref[...]
    bim = bim_ref[...]
    are_ref[...] = lbar_re
    aim_ref[...] = lbar_im
    bbre_ref[...] = coef_re * bre - coef_im * bim
    bbim_ref[...] = coef_re * bim + coef_im * bre


def _s5_discretise(lam_re, lam_im, log_dt, b_re, b_im):
    g, p = lam_re.shape
    hg = b_re.shape[-1]
    n = g * p
    row = lambda a: a.reshape(1, n)
    tr = lambda a: a.reshape(n, hg).T
    ldt = jnp.broadcast_to(log_dt[:, None], (g, p))
    return pl.pallas_call(
        _s5_disc_kernel,
        out_shape=[
            jax.ShapeDtypeStruct((1, n), F32), jax.ShapeDtypeStruct((1, n), F32),
            jax.ShapeDtypeStruct((hg, n), F32), jax.ShapeDtypeStruct((hg, n), F32),
        ],
        name="s5_discretise",
    )(row(lam_re), row(lam_im), row(ldt), tr(b_re), tr(b_im))


def _s5_kernel(u_ref, zs_ref, perm_ref, are_ref, aim_ref, wb_ref, wc_ref, d_ref, wglu_ref, bglu_ref, o_ref,
               x_scr, st_scr, *, bsz, steps, n_chunks):
    @pl.when(pl.program_id(0) == 0)
    def _():
        st_scr[...] = jnp.zeros_like(st_scr)

    cs = CHUNK_STATE
    perm = perm_ref[...]
    perm_steps = perm.shape[0] // bsz
    pm = bsz * perm_steps
    u = jnp.concatenate(
        [jnp.dot(perm, u_ref[:, t0:t0 + perm_steps, :].reshape(pm, u_ref.shape[2]),
                 preferred_element_type=F32) for t0 in range(0, steps, perm_steps)], axis=0).astype(BF16)

    def project_in(c):
        x_scr[:, 2 * cs * c:2 * cs * (c + 1)] = jnp.dot(
            u[:, CHUNK_IN * c:CHUNK_IN * (c + 1)], wb_ref[c], preferred_element_type=F32)

    def scan(c):
        re0 = 2 * cs * c
        im0 = re0 + cs
        a_re = jnp.broadcast_to(are_ref[:, cs * c:cs * (c + 1)], (bsz, cs))
        a_im = jnp.broadcast_to(aim_ref[:, cs * c:cs * (c + 1)], (bsz, cs))

        def step(t, carry):
            xr, xi = carry
            r0 = pl.multiple_of(t * bsz, bsz)
            nr = a_re * xr - a_im * xi + x_scr[pl.ds(r0, bsz), re0:re0 + cs]
            ni = a_re * xi + a_im * xr + x_scr[pl.ds(r0, bsz), im0:im0 + cs]
            x_scr[pl.ds(r0, bsz), re0:re0 + cs] = nr
            x_scr[pl.ds(r0, bsz), im0:im0 + cs] = ni
            return nr, ni

        xr, xi = lax.fori_loop(0, steps, step,
                               (st_scr[:, re0:re0 + cs], st_scr[:, im0:im0 + cs]), unroll=True)
        st_scr[:, re0:re0 + cs] = xr
        st_scr[:, im0:im0 + cs] = xi

    def project_out(c):
        return jnp.dot(x_scr[:, 2 * cs * c:2 * cs * (c + 1)].astype(BF16), wc_ref[c],
                       preferred_element_type=F32)

    ys = [None] * n_chunks
    project_in(0)
    for c in range(n_chunks):
        if c + 1 < n_chunks:
            project_in(c + 1)
        if c >= 1:
            ys[c - 1] = project_out(c - 1)
        scan(c)
    ys[n_chunks - 1] = project_out(n_chunks - 1)
    y = jnp.concatenate(ys, axis=1) + d_ref[...] * u.astype(F32)
    y = jax.nn.gelu(y)
    z = jnp.dot(y.astype(BF16), wglu_ref[...], preferred_element_type=F32) + bglu_ref[...]
    out = (y * _sigmoid(z)).astype(BF16)
    for i, t0 in enumerate(range(0, steps, perm_steps)):
        part = lax.dot_general(perm, out[i * pm:(i + 1) * pm, :], (((0,), (0,)), ((), ())),
                               preferred_element_type=F32)
        o_ref[:, t0:t0 + perm_steps, :] = (part.astype(BF16).reshape(bsz, perm_steps, part.shape[1])
                                           * zs_ref[:, t0:t0 + perm_steps, :])


def _s5(u_bt, zs_bt, lbar_re, lbar_im, wb, wc, d_row, w_glu, b_glu, bsz, seq):
    ssm_w = u_bt.shape[2]
    n_chunks = wb.shape[0]
    steps = S5_STEPS
    m = steps * bsz
    n_state = n_chunks * CHUNK_STATE
    kern = functools.partial(_s5_kernel, bsz=bsz, steps=steps, n_chunks=n_chunks)
    const2 = lambda i: (0, 0)
    const3 = lambda i: (0, 0, 0)
    rows = pl.BlockSpec((bsz, steps, ssm_w), lambda i: (0, i, 0))
    pm = bsz * S5_PERM_STEPS
    r = jnp.arange(pm)
    perm = (r[None, :] == ((r % bsz) * S5_PERM_STEPS + r // bsz)[:, None]).astype(BF16)
    return pl.pallas_call(
        kern,
        grid=(seq // steps,),
        in_specs=[
            rows,
            rows,
            pl.BlockSpec((pm, pm), const2),
            pl.BlockSpec((1, n_state), const2),
            pl.BlockSpec((1, n_state), const2),
            pl.BlockSpec(wb.shape, const3),
            pl.BlockSpec(wc.shape, const3),
            pl.BlockSpec((1, ssm_w), const2),
            pl.BlockSpec(w_glu.shape, const2),
            pl.BlockSpec((1, ssm_w), const2),
        ],
        out_specs=rows,
        out_shape=jax.ShapeDtypeStruct((bsz, seq, ssm_w), BF16),
        scratch_shapes=[
            pltpu.VMEM((m, 2 * n_state), F32),
            pltpu.VMEM((bsz, 2 * n_state), F32),
        ],
        compiler_params=pltpu.CompilerParams(
            dimension_semantics=("arbitrary",), vmem_limit_bytes=VMEM_LIMIT),
        name="s5_scan",
    )(u_bt, zs_bt, perm, lbar_re, lbar_im, wb, wc, d_row, w_glu, b_glu)


def _s5_block_diag_weights(bbar_re, bbar_im, c_re, c_im):
    hg, n = bbar_re.shape
    g = n // SSM_STATE
    nc = g // GROUPS_PER_CHUNK
    eye = jnp.eye(GROUPS_PER_CHUNK, dtype=F32)

    def wb_part(bb):
        t = bb.reshape(hg, nc, GROUPS_PER_CHUNK, SSM_STATE)
        w = jnp.einsum('ab,hcbp->cahbp', eye, t)
        return w.reshape(nc, GROUPS_PER_CHUNK * hg, GROUPS_PER_CHUNK * SSM_STATE)

    def wc_part(cc):
        t = cc.reshape(nc, GROUPS_PER_CHUNK, hg, SSM_STATE)
        w = jnp.einsum('ab,cbhp->capbh', eye, t)
        return w.reshape(nc, GROUPS_PER_CHUNK * SSM_STATE, GROUPS_PER_CHUNK * hg)

    wb = jnp.concatenate([wb_part(bbar_re), wb_part(bbar_im)], axis=2).astype(BF16)
    wc = jnp.concatenate([wc_part(c_re), -wc_part(c_im)], axis=1).astype(BF16)
    return wb, wc


def _out_kernel(att_ref, ssm_ref, gate_ref, x_ref, woa_ref, wos_ref, wout_ref, fg_ref, o_ref, *, d_model):
    ya = jnp.dot(att_ref[...], woa_ref[...], preferred_element_type=F32)
    ys = jnp.dot(ssm_ref[...], wos_ref[...], preferred_element_type=F32)
    ga = gate_ref[:, 0:d_model].astype(F32)
    gs = gate_ref[:, d_model:2 * d_model].astype(F32)
    merged = ga * ya + gs * ys
    o = x_ref[...] + jnp.dot(merged.astype(BF16), wout_ref[...], preferred_element_type=F32)
    ms = jnp.mean(o * o, axis=-1, keepdims=True)
    o_ref[...] = o * lax.rsqrt(ms + EPS) * fg_ref[...]


def _out_proj(att, ssm, gate, x2, w_o_att, w_o_ssm, w_out, final_g, bsz, seq):
    d_model = x2.shape[1]
    att_w = att.shape[1]
    ssm_w = w_o_ssm.shape[0]
    tm = ROW_TILE
    nl = seq // tm
    row = lambda b, l: (b * nl + l, 0)
    const = lambda b, l: (0, 0)
    kern = functools.partial(_out_kernel, d_model=d_model)
    return pl.pallas_call(
        kern,
        grid=(bsz, nl),
        in_specs=[
            pl.BlockSpec((tm, att_w), row),
            pl.BlockSpec((tm, ssm_w), row),
            pl.BlockSpec((tm, 2 * d_model), row),
            pl.BlockSpec((tm, d_model), row),
            pl.BlockSpec(w_o_att.shape, const),
            pl.BlockSpec(w_o_ssm.shape, const),
            pl.BlockSpec(w_out.shape, const),
            pl.BlockSpec((1, d_model), const),
        ],
        out_specs=pl.BlockSpec((tm, d_model), row),
        out_shape=jax.ShapeDtypeStruct((bsz * seq, d_model), F32),
        compiler_params=pltpu.CompilerParams(
            dimension_semantics=("arbitrary", "arbitrary"), vmem_limit_bytes=VMEM_LIMIT),
        name="out_proj",
    )(att, ssm, gate, x2, w_o_att, w_o_ssm, w_out, final_g)


def kernel(x, norm_g, w_in, lambda_q1, lambda_k1, lambda_q2, lambda_k2, subln_g, w_o_att,
           ssm_lambda_re, ssm_lambda_im, ssm_log_dt, ssm_b_re, ssm_b_im, ssm_c_re, ssm_c_im,
           ssm_d, w_glu, b_glu, w_o_ssm, w_out, final_g):
    bsz, seq, d_model = x.shape
    depth = w_in.shape[0]
    assert depth == 1, "single-layer block"
    att_w = w_o_att.shape[1]
    ssm_w = w_o_ssm.shape[1]
    assert att_w == ATT_HEADS * ATT_V_DIM
    assert w_in.shape[2] == 4 * att_w + 2 * ssm_w + 2 * d_model
    assert seq % ROW_TILE == 0 and seq % ATT_BLOCK == 0 and seq % S5_STEPS == 0
    assert S5_STEPS % S5_PERM_STEPS == 0
    assert bsz % 16 == 0, "batch rows fill whole bf16 sublane tiles in the S5 kernel"
    assert (ssm_w // SSM_GROUP) % GROUPS_PER_CHUNK == 0

    x2 = x.reshape(bsz * seq, d_model)
    qkv, za, u, zs, gate = _in_proj(x2, norm_g[0][None], w_in[0].astype(BF16), bsz, seq, att_w, ssm_w)

    slopes = jnp.asarray([2.0 ** (-8.0 * (h + 1) / ATT_HEADS) for h in range(ATT_HEADS)], F32)
    att = _attention(qkv, za, slopes, lambda_q1, lambda_k1, lambda_q2, lambda_k2, subln_g, bsz, seq)

    lbar_re, lbar_im, bbar_re, bbar_im = _s5_discretise(
        ssm_lambda_re[0], ssm_lambda_im[0], ssm_log_dt[0], ssm_b_re[0], ssm_b_im[0])
    wb, wc = _s5_block_diag_weights(bbar_re, bbar_im, ssm_c_re[0], ssm_c_im[0])
    ssm = _s5(u.reshape(bsz, seq, ssm_w), zs.reshape(bsz, seq, ssm_w), lbar_re, lbar_im, wb, wc,
              ssm_d[0].reshape(1, ssm_w), w_glu[0].astype(BF16), b_glu[0][None], bsz, seq)

    out = _out_proj(att, ssm.reshape(bsz * seq, ssm_w), gate, x2,
                    w_o_att[0].astype(BF16), w_o_ssm[0].astype(BF16), w_out[0].astype(BF16),
                    final_g[None], bsz, seq)
    return out.reshape(bsz, seq, d_model)
```

```python
import functools
import math

import jax
import jax.numpy as jnp
from jax import lax
from jax.experimental import pallas as pl
from jax.experimental.pallas import tpu as pltpu

EPS = 1e-5
ATT_HEADS = 8
ATT_HEAD_DIM = 64
ATT_V_DIM = 2 * ATT_HEAD_DIM
SSM_GROUP = 16
SSM_STATE = 64
LAMBDA_INIT = 0.8 - 0.6 * math.exp(-0.3 * 0)
LOG2E = math.log2(math.e)
Q_PRESCALE = ATT_HEAD_DIM ** -0.5 * LOG2E
BIAS_TERMS = 3

GROUPS_PER_CHUNK = 8
CHUNK_IN = GROUPS_PER_CHUNK * SSM_GROUP
CHUNK_STATE = GROUPS_PER_CHUNK * SSM_STATE

ROW_TILE = 1024
OUT_ROW_GROUPS = 4
ATT_BLOCK = 128
ATT_SUM_ROWS = 16
ATT_VALUE_LAG = 7
S5_STEPS = 64
S5_PERM_STEPS = 32
VMEM_LIMIT = 56 * 1024 * 1024

F32 = jnp.float32
BF16 = jnp.bfloat16


def _sigmoid(x):
    return 0.5 * jnp.tanh(0.5 * x) + 0.5


def _silu(x):
    return x * _sigmoid(x)


def _in_proj_kernel(x_ref, g_ref, w_ref, qkv_ref, za_ref, u_ref, zs_ref, gate_ref, *, att_w, ssm_w, d_model):
    x = x_ref[...]
    ms = jnp.mean(x * x, axis=-1, keepdims=True)
    h = (x * lax.rsqrt(ms + EPS) * g_ref[...]).astype(BF16)

    def proj(dst_ref, dst0, src0, width, fn=None, step=512):
        for o in range(0, width, step):
            r = jnp.dot(h, w_ref[:, src0 + o:src0 + o + step], preferred_element_type=F32)
            if fn is not None:
                r = fn(r)
            dst_ref[:, dst0 + o:dst0 + o + step] = r.astype(dst_ref.dtype)

    q0, z0 = 0, 3 * att_w
    u0 = z0 + att_w
    r0 = u0 + ssm_w
    proj(za_ref, 0, z0, att_w, fn=_silu)
    proj(zs_ref, 0, r0, ssm_w, fn=_silu)
    proj(gate_ref, 0, r0 + ssm_w, 2 * d_model, fn=_sigmoid)
    proj(qkv_ref, 0, q0, att_w, fn=lambda r: r * Q_PRESCALE)
    proj(u_ref, 0, u0, ssm_w)
    proj(qkv_ref, att_w, q0 + att_w, 2 * att_w)


def _in_proj(x2, norm_g, w_in, bsz, seq, att_w, ssm_w):
    d_model = x2.shape[1]
    tm = ROW_TILE
    nl = seq // tm
    kern = functools.partial(_in_proj_kernel, att_w=att_w, ssm_w=ssm_w, d_model=d_model)
    row = lambda b, l: (b * nl + l, 0)
    return pl.pallas_call(
        kern,
        grid=(bsz, nl),
        in_specs=[
            pl.BlockSpec((tm, d_model), row),
            pl.BlockSpec((1, d_model), lambda b, l: (0, 0)),
            pl.BlockSpec(w_in.shape, lambda b, l: (0, 0), pipeline_mode=pl.Buffered(1)),
        ],
        out_specs=[
            pl.BlockSpec((tm, 3 * att_w), row),
            pl.BlockSpec((tm, att_w), row),
            pl.BlockSpec((tm, ssm_w), row),
            pl.BlockSpec((tm, ssm_w), row),
            pl.BlockSpec((tm, 2 * d_model), row),
        ],
        out_shape=[
            jax.ShapeDtypeStruct((bsz * seq, 3 * att_w), BF16),
            jax.ShapeDtypeStruct((bsz * seq, att_w), BF16),
            jax.ShapeDtypeStruct((bsz * seq, ssm_w), BF16),
            jax.ShapeDtypeStruct((bsz * seq, ssm_w), BF16),
            jax.ShapeDtypeStruct((bsz * seq, 2 * d_model), BF16),
        ],
        compiler_params=pltpu.CompilerParams(
            dimension_semantics=("arbitrary", "arbitrary"), vmem_limit_bytes=VMEM_LIMIT),
        name="in_proj",
    )(x2, norm_g, w_in)


def _split_bf16(v, n):
    parts = []
    for _ in range(n):
        p = v.astype(BF16)
        parts.append(p)
        v = v - p.astype(F32)
    return parts


def _attn_kernel(slopes_ref, zero_ref, lq1_ref, lk1_ref, lq2_ref, lk2_ref, sg_ref, q_ref, k_ref, v_ref,
                 za_ref, o_ref, kx_ref, vt_ref, *sp_refs):
    blk = ATT_BLOCK
    hd = ATT_HEAD_DIM
    vd = ATT_V_DIM
    n_blk = k_ref.shape[0] // blk
    s_refs, p_refs = sp_refs[:n_blk], sp_refs[n_blk:]

    slope2 = slopes_ref[pl.program_id(1)] * LOG2E
    lane = lax.broadcasted_iota(jnp.int32, (blk, vd), 1)
    pos = lax.broadcasted_iota(jnp.int32, (blk, vd), 0)
    local = jnp.zeros((blk, vd), F32)
    for t, p in enumerate(_split_bf16(slope2 * pos.astype(F32), BIAS_TERMS)):
        local = jnp.where(lane == t, p.astype(F32), local)
    lane_row = lax.broadcasted_iota(jnp.int32, (1, vd), 1)
    for j in range(n_blk):
        rows = pl.ds(j * blk, blk)
        offset = jnp.zeros((1, vd), F32)
        for t, p in enumerate(_split_bf16(slope2 * jnp.full((1, vd), j * blk, F32), BIAS_TERMS)):
            offset = jnp.where(lane_row == BIAS_TERMS + t, p.astype(F32), offset)
        kx_ref[rows, 0:vd] = k_ref[rows, :]
        kx_ref[rows, vd:2 * vd] = (local + offset).astype(BF16)
        vt_ref[0:vd, j * blk:(j + 1) * blk] = v_ref[rows, :].astype(F32).T.astype(BF16)
    sub = lax.broadcasted_iota(jnp.int32, (ATT_SUM_ROWS, vt_ref.shape[1]), 0)
    vt_ref[vd:vd + ATT_SUM_ROWS, :] = jnp.where(sub == 0, 1.0, 0.0).astype(BF16)

    e1 = jnp.exp(jnp.sum(lq1_ref[...] * lk1_ref[...], axis=-1, keepdims=True))
    e2 = jnp.exp(jnp.sum(lq2_ref[...] * lk2_ref[...], axis=-1, keepdims=True))
    lam = e1 - e2 + LAMBDA_INIT
    gain = sg_ref[...] * (1.0 - LAMBDA_INIT)
    nt = (((1,), (1,)), ((), ()))

    key = lax.broadcasted_iota(jnp.int32, (blk, 2 * blk), 0)
    qry = lax.broadcasted_iota(jnp.int32, (blk, 2 * blk), 1)
    causal = key <= jnp.where(qry >= blk, qry - blk, qry)

    dyn0 = zero_ref[0]
    col_max = {}

    def score_stage(c):
        nk = (c + 1) * blk
        q = q_ref[c * blk:(c + 1) * blk, :].astype(F32)
        zero = jnp.zeros_like(q)
        ones = jnp.where(lane < 2 * BIAS_TERMS, jnp.ones_like(q), zero)
        qx = jnp.concatenate([
            jnp.concatenate([jnp.where(lane < hd, q, zero), ones], axis=1),
            jnp.concatenate([jnp.where(lane >= hd, q, zero), ones], axis=1)], axis=0).astype(BF16)
        s = lax.dot_general(kx_ref[0:nk, :], qx, nt, preferred_element_type=F32)
        m = None
        for j in range(c + 1):
            sj = s[j * blk:(j + 1) * blk, :]
            if j == c:
                sj = jnp.where(causal, sj, -jnp.inf)
            s_refs[c][j * blk:(j + 1) * blk, :] = sj
            mj = jnp.max(sj, axis=0, keepdims=True)
            m = mj if m is None else jnp.maximum(m, mj)
        col_max[c] = m

    def softmax_stage(c):
        for j in range(c + 1):
            sj = s_refs[c][pl.ds(pl.multiple_of(dyn0 + j * blk, blk), blk), :]
            p_refs[c][j * blk:(j + 1) * blk, :] = jnp.exp2(sj - col_max[c]).astype(BF16)

    def value_stage(c):
        nk = (c + 1) * blk
        p = p_refs[c][pl.ds(pl.multiple_of(dyn0, blk), nk), :]
        acc = jnp.dot(vt_ref[:, 0:nk], p, preferred_element_type=F32)
        r = 1.0 / acc[vd:vd + 1, :]
        ot = acc[0:vd, 0:blk] * r[:, 0:blk] - acc[0:vd, blk:2 * blk] * (lam * r[:, blk:2 * blk])
        ms = jnp.mean(ot * ot, axis=0, keepdims=True)
        att = (ot * lax.rsqrt(ms + EPS)).T * gain
        o_ref[c * blk:(c + 1) * blk, :] = att.astype(BF16) * za_ref[c * blk:(c + 1) * blk, :]

    order = list(reversed(range(n_blk)))
    lag = lambda i: ATT_VALUE_LAG if i < 4 else ATT_VALUE_LAG - 2
    emitted = 0
    for t in range(n_blk):
        score_stage(order[t])
        softmax_stage(order[t])
        while emitted < n_blk and emitted + lag(emitted) <= t:
            value_stage(order[emitted])
            emitted += 1
    while emitted < n_blk:
        value_stage(order[emitted])
        emitted += 1


def _attention(qkv, za, slopes, lq1, lk1, lq2, lk2, subln_g, bsz, seq):
    blk = ATT_BLOCK
    n_blk = seq // blk
    nh = ATT_HEADS
    vec = pl.BlockSpec((1, ATT_HEAD_DIM), lambda b, h: (0, 0))
    head = lambda first: pl.BlockSpec((seq, ATT_V_DIM), lambda b, h: (b, first + h))
    return pl.pallas_call(
        _attn_kernel,
        grid=(bsz, nh),
        in_specs=[
            pl.BlockSpec(memory_space=pltpu.SMEM),
            pl.BlockSpec(memory_space=pltpu.SMEM),
            vec, vec, vec, vec,
            pl.BlockSpec((1, ATT_V_DIM), lambda b, h: (0, 0)),
            head(0), head(nh), head(2 * nh),
            head(0),
        ],
        out_specs=head(0),
        out_shape=jax.ShapeDtypeStruct((bsz * seq, nh * ATT_V_DIM), BF16),
        scratch_shapes=[
            pltpu.VMEM((seq, 2 * ATT_V_DIM), BF16),
            pltpu.VMEM((ATT_V_DIM + ATT_SUM_ROWS, seq), BF16),
        ] + [pltpu.VMEM(((c + 1) * blk, 2 * blk), F32) for c in range(n_blk)]
          + [pltpu.VMEM(((c + 1) * blk, 2 * blk), BF16) for c in range(n_blk)],
        compiler_params=pltpu.CompilerParams(
            dimension_semantics=("arbitrary", "arbitrary"), vmem_limit_bytes=VMEM_LIMIT),
        name="diff_attention",
    )(slopes, jnp.zeros((1,), jnp.int32), lq1, lk1, lq2, lk2, subln_g, qkv, qkv, qkv, za)


def _s5_disc_kernel(lre_ref, lim_ref, ldt_ref, bre_ref, bim_ref, are_ref, aim_ref, bbre_ref, bbim_ref):
    dt = jnp.exp(ldt_ref[...])
    lre = jnp.minimum(lre_ref[...], -1e-4)
    lim = lim_ref[...]
    mag = jnp.exp(lre * dt)
    lbar_re = mag * jnp.cos(lim * dt)
    lbar_im = mag * jnp.sin(lim * dt)
    num_re = lbar_re - 1.0
    den = lre * lre + lim * lim
    coef_re = (num_re * lre + lbar_im * lim) / den
    coef_im = (lbar_im * lre - num_re * lim) / den
    bre = bre_ref[...]
    bim = bim_ref[...]
    are_ref[...] = lbar_re
    aim_ref[...] = lbar_im
    bbre_ref[...] = coef_re * bre - coef_im * bim
    bbim_ref[...] = coef_re * bim + coef_im * bre


def _s5_discretise(lam_re, lam_im, log_dt, b_re, b_im):
    g, p = lam_re.shape
    hg = b_re.shape[-1]
    n = g * p
    row = lambda a: a.reshape(1, n)
    tr = lambda a: a.reshape(n, hg).T
    ldt = jnp.broadcast_to(log_dt[:, None], (g, p))
    return pl.pallas_call(
        _s5_disc_kernel,
        out_shape=[
            jax.ShapeDtypeStruct((1, n), F32), jax.ShapeDtypeStruct((1, n), F32),
            jax.ShapeDtypeStruct((hg, n), F32), jax.ShapeDtypeStruct((hg, n), F32),
        ],
        name="s5_discretise",
    )(row(lam_re), row(lam_im), row(ldt), tr(b_re), tr(b_im))


def _s5_kernel(u_ref, zs_ref, perm_ref, are_ref, aim_ref, wb_ref, wc_ref, d_ref, wglu_ref, bglu_ref, o_ref,
               x_scr, st_scr, *, bsz, steps, n_chunks):
    @pl.when(pl.program_id(0) == 0)
    def _():
        st_scr[...] = jnp.zeros_like(st_scr)

    cs = CHUNK_STATE
    perm = perm_ref[...]
    perm_steps = perm.shape[0] // bsz
    pm = bsz * perm_steps
    u = jnp.concatenate(
        [jnp.dot(perm, u_ref[:, t0:t0 + perm_steps, :].reshape(pm, u_ref.shape[2]),
                 preferred_element_type=F32) for t0 in range(0, steps, perm_steps)], axis=0).astype(BF16)

    def project_in(c):
        x_scr[:, 2 * cs * c:2 * cs * (c + 1)] = jnp.dot(
            u[:, CHUNK_IN * c:CHUNK_IN * (c + 1)], wb_ref[c], preferred_element_type=F32)

    def scan(c):
        re0 = 2 * cs * c
        im0 = re0 + cs
        a_re = jnp.broadcast_to(are_ref[:, cs * c:cs * (c + 1)], (bsz, cs))
        a_im = jnp.broadcast_to(aim_ref[:, cs * c:cs * (c + 1)], (bsz, cs))

        def step(t, carry):
            xr, xi = carry
            r0 = pl.multiple_of(t * bsz, bsz)
            nr = a_re * xr - a_im * xi + x_scr[pl.ds(r0, bsz), re0:re0 + cs]
            ni = a_re * xi + a_im * xr + x_scr[pl.ds(r0, bsz), im0:im0 + cs]
            x_scr[pl.ds(r0, bsz), re0:re0 + cs] = nr
            x_scr[pl.ds(r0, bsz), im0:im0 + cs] = ni
            return nr, ni

        xr, xi = lax.fori_loop(0, steps, step,
                               (st_scr[:, re0:re0 + cs], st_scr[:, im0:im0 + cs]), unroll=True)
        st_scr[:, re0:re0 + cs] = xr
        st_scr[:, im0:im0 + cs] = xi

    def project_out(c):
        return jnp.dot(x_scr[:, 2 * cs * c:2 * cs * (c + 1)].astype(BF16), wc_ref[c],
                       preferred_element_type=F32)

    for c in range(n_chunks):
        project_in(c)
    for c in range(n_chunks):
        scan(c)
    ys = [project_out(c) for c in range(n_chunks)]
    y = jnp.concatenate(ys, axis=1) + d_ref[...] * u.astype(F32)
    y = jax.nn.gelu(y)
    z = jnp.dot(y.astype(BF16), wglu_ref[...], preferred_element_type=F32) + bglu_ref[...]
    out = (y * _sigmoid(z)).astype(BF16)
    for i, t0 in enumerate(range(0, steps, perm_steps)):
        part = lax.dot_general(perm, out[i * pm:(i + 1) * pm, :], (((0,), (0,)), ((), ())),
                               preferred_element_type=F32)
        o_ref[:, t0:t0 + perm_steps, :] = (part.astype(BF16).reshape(bsz, perm_steps, part.shape[1])
                                           * zs_ref[:, t0:t0 + perm_steps, :])


def _s5(u_bt, zs_bt, lbar_re, lbar_im, wb, wc, d_row, w_glu, b_glu, bsz, seq):
    ssm_w = u_bt.shape[2]
    n_chunks = wb.shape[0]
    steps = S5_STEPS
    m = steps * bsz
    n_state = n_chunks * CHUNK_STATE
    kern = functools.partial(_s5_kernel, bsz=bsz, steps=steps, n_chunks=n_chunks)
    const2 = lambda i: (0, 0)
    const3 = lambda i: (0, 0, 0)
    rows = pl.BlockSpec((bsz, steps, ssm_w), lambda i: (0, i, 0))
    pm = bsz * S5_PERM_STEPS
    r = jnp.arange(pm)
    perm = (r[None, :] == ((r % bsz) * S5_PERM_STEPS + r // bsz)[:, None]).astype(BF16)
    return pl.pallas_call(
        kern,
        grid=(seq // steps,),
        in_specs=[
            rows,
            rows,
            pl.BlockSpec((pm, pm), const2),
            pl.BlockSpec((1, n_state), const2),
            pl.BlockSpec((1, n_state), const2),
            pl.BlockSpec(wb.shape, const3),
            pl.BlockSpec(wc.shape, const3),
            pl.BlockSpec((1, ssm_w), const2),
            pl.BlockSpec(w_glu.shape, const2),
            pl.BlockSpec((1, ssm_w), const2),
        ],
        out_specs=rows,
        out_shape=jax.ShapeDtypeStruct((bsz, seq, ssm_w), BF16),
        scratch_shapes=[
            pltpu.VMEM((m, 2 * n_state), F32),
            pltpu.VMEM((bsz, 2 * n_state), F32),
        ],
        compiler_params=pltpu.CompilerParams(
            dimension_semantics=("arbitrary",), vmem_limit_bytes=VMEM_LIMIT),
        name="s5_scan",
    )(u_bt, zs_bt, perm, lbar_re, lbar_im, wb, wc, d_row, w_glu, b_glu)


def _s5_block_diag_weights(bbar_re, bbar_im, c_re, c_im):
    hg, n = bbar_re.shape
    g = n // SSM_STATE
    nc = g // GROUPS_PER_CHUNK
    eye = jnp.eye(GROUPS_PER_CHUNK, dtype=F32)

    def wb_part(bb):
        t = bb.reshape(hg, nc, GROUPS_PER_CHUNK, SSM_STATE)
        w = jnp.einsum('ab,hcbp->cahbp', eye, t)
        return w.reshape(nc, GROUPS_PER_CHUNK * hg, GROUPS_PER_CHUNK * SSM_STATE)

    def wc_part(cc):
        t = cc.reshape(nc, GROUPS_PER_CHUNK, hg, SSM_STATE)
        w = jnp.einsum('ab,cbhp->capbh', eye, t)
        return w.reshape(nc, GROUPS_PER_CHUNK * SSM_STATE, GROUPS_PER_CHUNK * hg)

    wb = jnp.concatenate([wb_part(bbar_re), wb_part(bbar_im)], axis=2).astype(BF16)
    wc = jnp.concatenate([wc_part(c_re), -wc_part(c_im)], axis=1).astype(BF16)
    return wb, wc


def _out_kernel(att_ref, ssm_ref, gate_ref, x_ref, woa_ref, wos_ref, wout_ref, fg_ref, o_ref, *, d_model):
    rows_per = att_ref.shape[0] // OUT_ROW_GROUPS
    groups = [slice(i * rows_per, (i + 1) * rows_per) for i in range(OUT_ROW_GROUPS)]
    merged = []
    for rows in groups:
        ya = jnp.dot(att_ref[rows, :], woa_ref[...], preferred_element_type=F32)
        ys = jnp.dot(ssm_ref[rows, :], wos_ref[...], preferred_element_type=F32)
        ga = gate_ref[rows, 0:d_model].astype(F32)
        gs = gate_ref[rows, d_model:2 * d_model].astype(F32)
        merged.append((ga * ya + gs * ys).astype(BF16))
    for rows, mg in zip(groups, merged):
        o = x_ref[rows, :] + jnp.dot(mg, wout_ref[...], preferred_element_type=F32)
        ms = jnp.mean(o * o, axis=-1, keepdims=True)
        o_ref[rows, :] = o * lax.rsqrt(ms + EPS) * fg_ref[...]


def _out_proj(att, ssm, gate, x2, w_o_att, w_o_ssm, w_out, final_g, bsz, seq):
    d_model = x2.shape[1]
    att_w = att.shape[1]
    ssm_w = w_o_ssm.shape[0]
    tm = ROW_TILE
    nl = seq // tm
    row = lambda b, l: (b * nl + l, 0)
    const = lambda b, l: (0, 0)
    kern = functools.partial(_out_kernel, d_model=d_model)
    return pl.pallas_call(
        kern,
        grid=(bsz, nl),
        in_specs=[
            pl.BlockSpec((tm, att_w), row),
            pl.BlockSpec((tm, ssm_w), row),
            pl.BlockSpec((tm, 2 * d_model), row),
            pl.BlockSpec((tm, d_model), row),
            pl.BlockSpec(w_o_att.shape, const),
            pl.BlockSpec(w_o_ssm.shape, const),
            pl.BlockSpec(w_out.shape, const),
            pl.BlockSpec((1, d_model), const),
        ],
        out_specs=pl.BlockSpec((tm, d_model), row),
        out_shape=jax.ShapeDtypeStruct((bsz * seq, d_model), F32),
        compiler_params=pltpu.CompilerParams(
            dimension_semantics=("arbitrary", "arbitrary"), vmem_limit_bytes=VMEM_LIMIT),
        name="out_proj",
    )(att, ssm, gate, x2, w_o_att, w_o_ssm, w_out, final_g)


def kernel(x, norm_g, w_in, lambda_q1, lambda_k1, lambda_q2, lambda_k2, subln_g, w_o_att,
           ssm_lambda_re, ssm_lambda_im, ssm_log_dt, ssm_b_re, ssm_b_im, ssm_c_re, ssm_c_im,
           ssm_d, w_glu, b_glu, w_o_ssm, w_out, final_g):
    bsz, seq, d_model = x.shape
    depth = w_in.shape[0]
    assert depth == 1, "single-layer block"
    att_w = w_o_att.shape[1]
    ssm_w = w_o_ssm.shape[1]
    assert att_w == ATT_HEADS * ATT_V_DIM
    assert w_in.shape[2] == 4 * att_w + 2 * ssm_w + 2 * d_model
    assert seq % ROW_TILE == 0 and seq % ATT_BLOCK == 0 and seq % S5_STEPS == 0
    assert S5_STEPS % S5_PERM_STEPS == 0
    assert bsz % 16 == 0, "batch rows fill whole bf16 sublane tiles in the S5 kernel"
    assert (ssm_w // SSM_GROUP) % GROUPS_PER_CHUNK == 0

    x2 = x.reshape(bsz * seq, d_model)
    qkv, za, u, zs, gate = _in_proj(x2, norm_g[0][None], w_in[0].astype(BF16), bsz, seq, att_w, ssm_w)

    slopes = jnp.asarray([2.0 ** (-8.0 * (h + 1) / ATT_HEADS) for h in range(ATT_HEADS)], F32)
    att = _attention(qkv, za, slopes, lambda_q1, lambda_k1, lambda_q2, lambda_k2, subln_g, bsz, seq)

    lbar_re, lbar_im, bbar_re, bbar_im = _s5_discretise(
        ssm_lambda_re[0], ssm_lambda_im[0], ssm_log_dt[0], ssm_b_re[0], ssm_b_im[0])
    wb, wc = _s5_block_diag_weights(bbar_re, bbar_im, ssm_c_re[0], ssm_c_im[0])
    ssm = _s5(u.reshape(bsz, seq, ssm_w), zs.reshape(bsz, seq, ssm_w), lbar_re, lbar_im, wb, wc,
              ssm_d[0].reshape(1, ssm_w), w_glu[0].astype(BF16), b_glu[0][None], bsz, seq)

    out = _out_proj(att, ssm.reshape(bsz * seq, ssm_w), gate, x2,
                    w_o_att[0].astype(BF16), w_o_ssm[0].astype(BF16), w_out[0].astype(BF16),
                    final_g[None], bsz, seq)
    return out.reshape(bsz, seq, d_model)
```
